```python
import jax, jax.numpy as jnp
from jax import lax
import numpy as np

D_MODEL = 1024
BATCH = 8
SEQ = 8192
DEPTH = 2

EPS = 1e-6
GLA_HEADS = 4
GLA_DK = 128
GLA_DV = 256
GLA_GATE_RANK = 16
GLA_GATE_TAU = 16.0
GLA_CHUNK = 64
GLA_PROJ = GLA_HEADS * (2 * GLA_DK + 2 * GLA_DV) + GLA_GATE_RANK
SWA_HEADS = 16
SWA_KV_HEADS = 4
SWA_GROUP = SWA_HEADS // SWA_KV_HEADS
SWA_HEAD_DIM = 64
SWA_WINDOW = 128
SWA_BLOCK = 128
ROT_DIM = SWA_HEAD_DIM // 4
ROPE_THETA = 500000.0
PEER_HEADS = 8
PEER_N_KEYS = 128
PEER_EXPERTS = PEER_N_KEYS * PEER_N_KEYS
PEER_TOPK = 16
PEER_QDIM = 256
PEER_BLOCK = 128

kernel_name = "yoco_gla_swa_sink_peer_block"


def rmsnorm(x, g):
    x32 = x.astype(jnp.float32)
    y = x32 * lax.rsqrt(jnp.mean(x32 * x32, axis=-1, keepdims=True) + EPS)
    return (y * g.astype(jnp.float32)).astype(x.dtype)


def modulate(x, g, shift, scale):
    return rmsnorm(x, g) * (1.0 + scale[:, None, :]) + shift[:, None, :]


def rope_tables(positions):
    inv = ROPE_THETA ** (-jnp.arange(0, ROT_DIM, 2, dtype=jnp.float32) / ROT_DIM)
    ang = positions.astype(jnp.float32)[..., None] * inv
    return jnp.cos(ang), jnp.sin(ang)


def apply_partial_rope(x, cos, sin):
    half = ROT_DIM // 2
    xr = x[..., :ROT_DIM].astype(jnp.float32)
    x1, x2 = xr[..., :half], xr[..., half:]
    c = cos[:, :, None, :]
    s = sin[:, :, None, :]
    rot = jnp.concatenate([x1 * c - x2 * s, x2 * c + x1 * s], axis=-1).astype(x.dtype)
    return jnp.concatenate([rot, x[..., ROT_DIM:]], axis=-1)


def gla_mixer(h, w_in, w_g2, b_g2, norm_g, w_out):
    bsz, s_len, _ = h.shape
    H, dk, dv, C = GLA_HEADS, GLA_DK, GLA_DV, GLA_CHUNK
    n_c = s_len // C
    proj = h @ w_in
    q, k, v, og, gl = jnp.split(proj, [H * dk, 2 * H * dk, 2 * H * dk + H * dv, 2 * H * dk + 2 * H * dv], axis=-1)
    log_a = jax.nn.log_sigmoid((gl @ w_g2 + b_g2).astype(jnp.float32)) / GLA_GATE_TAU
    shp = (bsz, n_c, C, H, dk)
    q = q.reshape(shp).astype(jnp.float32) * (dk ** -0.5)
    k = k.reshape(shp).astype(jnp.float32)
    v = v.reshape(bsz, n_c, C, H, dv).astype(jnp.float32)
    b = jnp.cumsum(log_a.reshape(shp), axis=2)
    b_last = b[:, :, -1:]
    q_t = q * jnp.exp(b)
    k_t = k * jnp.exp(-b)
    k_dec = k * jnp.exp(b_last - b)
    scores = jnp.einsum('bnchd,bnmhd->bnhcm', q_t, k_t)
    tri = jnp.tril(jnp.ones((C, C), dtype=bool))
    scores = jnp.where(tri, scores, 0.0)
    o_intra = jnp.einsum('bnhcm,bnmhe->bnche', scores, v)
    dec = jnp.exp(b_last[:, :, 0])

    def step(state, inp):
        q_n, k_n, v_n, d_n = inp
        o_n = jnp.einsum('bchd,bhde->bche', q_n, state)
        state = d_n[..., None] * state + jnp.einsum('bchd,bche->bhde', k_n, v_n)
        return state, o_n

    xs = (jnp.moveaxis(q_t, 1, 0), jnp.moveaxis(k_dec, 1, 0), jnp.moveaxis(v, 1, 0), jnp.moveaxis(dec, 1, 0))
    _, o_inter = lax.scan(step, jnp.zeros((bsz, H, dk, dv), jnp.float32), xs)
    o = (o_intra + jnp.moveaxis(o_inter, 0, 1)).reshape(bsz, s_len, H, dv)
    o = rmsnorm(o, norm_g) * jax.nn.silu(og.reshape(bsz, s_len, H, dv).astype(jnp.float32))
    return (o.reshape(bsz, s_len, H * dv) @ w_out).astype(h.dtype)


def shared_kv(h, kv_w, cos, sin):
    bsz, s_len, _ = h.shape
    kv = h @ kv_w
    k, v = jnp.split(kv, 2, axis=-1)
    k = apply_partial_rope(k.reshape(bsz, s_len, SWA_KV_HEADS, SWA_HEAD_DIM), cos, sin)
    v = v.reshape(bsz, s_len, SWA_KV_HEADS, SWA_HEAD_DIM)
    return k, v


def swa_sink_attention(h, w_q, sinks, w_out, k, v, cos, sin):
    bsz, s_len, _ = h.shape
    P = SWA_BLOCK
    n_b = s_len // P
    q = apply_partial_rope((h @ w_q).reshape(bsz, s_len, SWA_HEADS, SWA_HEAD_DIM), cos, sin)
    q = q.reshape(bsz, n_b, P, SWA_KV_HEADS, SWA_GROUP, SWA_HEAD_DIM)

    def band(t):
        prev = jnp.pad(t, ((0, 0), (P, 0), (0, 0), (0, 0)))[:, :s_len]
        shp = (bsz, n_b, P, SWA_KV_HEADS, SWA_HEAD_DIM)
        return jnp.concatenate([prev.reshape(shp), t.reshape(shp)], axis=2)

    k_band, v_band = band(k), band(v)
    s = jnp.einsum('bnqkgd,bnmkd->bnkgqm', q, k_band).astype(jnp.float32) * (SWA_HEAD_DIM ** -0.5)
    qi = jnp.arange(P)[:, None]
    mi = jnp.arange(2 * P)[None, :]
    rel = qi + P - mi
    in_window = (rel >= 0) & (rel < SWA_WINDOW)
    key_pos = jnp.arange(n_b)[:, None, None] * P - P + mi[None]
    mask = in_window[None] & (key_pos >= 0)
    s = jnp.where(mask[None, :, None, None], s, -jnp.inf)
    sink = sinks.astype(jnp.float32).reshape(SWA_KV_HEADS, SWA_GROUP)[None, None, :, :, None, None]
    m = jnp.maximum(jnp.max(s, axis=-1, keepdims=True), sink)
    p = jnp.exp(s - m)
    probs = p / (jnp.sum(p, axis=-1, keepdims=True) + jnp.exp(sink - m))
    o = jnp.einsum('bnkgqm,bnmkd->bnqkgd', probs, v_band.astype(jnp.float32))
    return (o.reshape(bsz, s_len, SWA_HEADS * SWA_HEAD_DIM) @ w_out).astype(h.dtype)


def peer(h, w_q, subkeys, u_tab, v_tab):
    bsz, s_len, d = h.shape
    half = PEER_QDIM // 2
    tokens = h.reshape(-1, PEER_BLOCK, d)

    def block(xb):
        q = (xb @ w_q).reshape(PEER_BLOCK, PEER_HEADS, PEER_QDIM)
        s1 = jnp.einsum('phd,nd->phn', q[..., :half], subkeys[0])
        s2 = jnp.einsum('phd,nd->phn', q[..., half:], subkeys[1])
        v1, i1 = lax.top_k(s1, PEER_TOPK)
        v2, i2 = lax.top_k(s2, PEER_TOPK)
        cand = (v1[..., :, None] + v2[..., None, :]).reshape(PEER_BLOCK, PEER_HEADS, PEER_TOPK * PEER_TOPK)
        cs, ci = lax.top_k(cand, PEER_TOPK)
        e = (jnp.take_along_axis(i1, ci // PEER_TOPK, axis=-1) * PEER_N_KEYS
             + jnp.take_along_axis(i2, ci % PEER_TOPK, axis=-1))
        g = jax.nn.softmax(cs.astype(jnp.float32), axis=-1)
        u = u_tab[e]
        a = jax.nn.gelu(jnp.einsum('pd,phkd->phk', xb, u).astype(jnp.float32), approximate=False)
        return jnp.einsum('phk,phkd->pd', g * a, v_tab[e].astype(jnp.float32)).astype(xb.dtype)

    return lax.map(block, tokens).reshape(bsz, s_len, d)


def setup_inputs(seed: int = 0) -> dict:
    key = jax.random.key(seed)
    ks = jax.random.split(key, 24)
    n_a = DEPTH // 2
    n_b = DEPTH - n_a
    D = D_MODEL
    f32 = jnp.float32

    def nrm(k, shape, scale):
        return jax.random.normal(k, shape, f32) * scale

    positions = (jax.random.randint(ks[2], (BATCH, 1), 0, 4096, dtype=jnp.int32)
                 + jnp.arange(SEQ, dtype=jnp.int32)[None, :])
    return {
        "x": nrm(ks[0], (BATCH, SEQ, D), 1.0),
        "c": nrm(ks[1], (BATCH, D), 1.0),
        "positions": positions,
        "mod_w": nrm(ks[3], (DEPTH, D, 6 * D), 0.5 * D ** -0.5),
        "mod_b": nrm(ks[4], (DEPTH, 6 * D), 0.02),
        "norm_g": 1.0 + nrm(ks[5], (DEPTH, 2, D), 0.02),
        "gla_w_in": nrm(ks[6], (n_a, D, GLA_PROJ), D ** -0.5),
        "gla_w_g2": nrm(ks[7], (n_a, GLA_GATE_RANK, GLA_HEADS * GLA_DK), GLA_GATE_RANK ** -0.5),
        "gla_b_g2": nrm(ks[8], (n_a, GLA_HEADS * GLA_DK), 0.1),
        "gla_norm_g": 1.0 + nrm(ks[9], (n_a, GLA_DV), 0.02),
        "gla_w_out": nrm(ks[10], (n_a, GLA_HEADS * GLA_DV, D), (GLA_HEADS * GLA_DV) ** -0.5),
        "kv_mod_w": nrm(ks[11], (D, 2 * D), 0.5 * D ** -0.5),
        "kv_mod_b": nrm(ks[12], (2 * D,), 0.02),
        "kv_norm_g": 1.0 + nrm(ks[13], (D,), 0.02),
        "kv_w": nrm(ks[14], (D, 2 * SWA_KV_HEADS * SWA_HEAD_DIM), D ** -0.5),
        "swa_w_q": nrm(ks[15], (n_b, D, SWA_HEADS * SWA_HEAD_DIM), D ** -0.5),
        "swa_sinks": nrm(ks[16], (n_b, SWA_HEADS), 0.5),
        "swa_w_out": nrm(ks[17], (n_b, SWA_HEADS * SWA_HEAD_DIM, D), (SWA_HEADS * SWA_HEAD_DIM) ** -0.5),
        "peer_w_q": nrm(ks[18], (DEPTH, D, PEER_HEADS * PEER_QDIM), D ** -0.5),
        "peer_subkeys": nrm(ks[19], (DEPTH, 2, PEER_N_KEYS, PEER_QDIM // 2), (PEER_QDIM // 2) ** -0.5),
        "peer_u": nrm(ks[20], (DEPTH, PEER_EXPERTS, D), D ** -0.5),
        "peer_v": nrm(ks[21], (DEPTH, PEER_EXPERTS, D), PEER_HEADS ** -0.5),
        "final_norm_g": 1.0 + nrm(ks[22], (D,), 0.02),
    }


def reference(x, c, positions, mod_w, mod_b, norm_g, gla_w_in, gla_w_g2, gla_b_g2, gla_norm_g,
              gla_w_out, kv_mod_w, kv_mod_b, kv_norm_g, kv_w, swa_w_q, swa_sinks, swa_w_out,
              peer_w_q, peer_subkeys, peer_u, peer_v, final_norm_g):
    n_a = DEPTH // 2
    cos, sin = rope_tables(positions)
    c_act = jax.nn.silu(c)
    k_sh, v_sh = None, None
    for l in range(DEPTH):
        mod = c_act @ mod_w[l] + mod_b[l]
        sh1, sc1, g1, sh2, sc2, g2 = jnp.split(mod, 6, axis=-1)
        h = modulate(x, norm_g[l, 0], sh1, sc1)
        if l < n_a:
            y = gla_mixer(h, gla_w_in[l], gla_w_g2[l], gla_b_g2[l], gla_norm_g[l], gla_w_out[l])
        else:
            j = l - n_a
            y = swa_sink_attention(h, swa_w_q[j], swa_sinks[j], swa_w_out[j], k_sh, v_sh, cos, sin)
        x = x + (g1[:, None, :] * y).astype(x.dtype)
        h = modulate(x, norm_g[l, 1], sh2, sc2)
        y = peer(h, peer_w_q[l], peer_subkeys[l], peer_u[l], peer_v[l])
        x = x + (g2[:, None, :] * y).astype(x.dtype)
        if l == n_a - 1:
            kv_mod = c_act @ kv_mod_w + kv_mod_b
            kv_shift, kv_scale = jnp.split(kv_mod, 2, axis=-1)
            k_sh, v_sh = shared_kv(modulate(x, kv_norm_g, kv_shift, kv_scale), kv_w, cos, sin)
    return rmsnorm(x, final_norm_g)
```

```python
import functools

import jax
import jax.numpy as jnp
from jax import lax
from jax.experimental import pallas as pl
from jax.experimental.pallas import tpu as pltpu

F32 = jnp.float32
MXU_DTYPE = jnp.bfloat16
EPS = 1e-6
NEG_INF = float("-inf")
POS_INF = float("inf")

LANES = 128
VMEM_LIMIT_BYTES = 56 * 1024 * 1024

GLA_HEADS, GLA_DK, GLA_DV, GLA_RANK, GLA_TAU, GLA_CHUNK = 4, 128, 256, 16, 16.0, 64
SWA_HEADS, SWA_KV_HEADS, SWA_HD, SWA_WINDOW = 16, 4, 64, 128
ROT_DIM, ROPE_THETA = 16, 500000.0
PEER_HEADS, PEER_KEYS, PEER_TOPK, PEER_QDIM = 8, 128, 16, 256

TB_PROJ = 512
TB_GLA = 256
TB_SWA = 512
TB_PEER = 512
EC_PEER = 1024

_NT = (((1,), (1,)), ((), ()))
_TN = (((0,), (0,)), ((), ()))


def _cparams(*sem):
    return pltpu.CompilerParams(dimension_semantics=sem, vmem_limit_bytes=VMEM_LIMIT_BYTES)


def _rms_modulate(x, g, shift, scale):
    y = x * lax.rsqrt(jnp.mean(x * x, axis=-1, keepdims=True) + EPS) * g
    return y * (1.0 + scale) + shift


def _dot(a, b):
    return jnp.dot(a.astype(MXU_DTYPE), b.astype(MXU_DTYPE), preferred_element_type=F32)


def _dot_nt(a, b):
    return lax.dot_general(a.astype(MXU_DTYPE), b.astype(MXU_DTYPE), _NT, preferred_element_type=F32)


def _mod_kernel(c_ref, w_ref, b_ref, o_ref):
    c = c_ref[...]
    ca = c * jax.nn.sigmoid(c)
    o_ref[...] = jnp.dot(ca, w_ref[...], preferred_element_type=F32,
                         precision=lax.Precision.HIGHEST) + b_ref[...]


def _mod_vectors(c, w, b):
    bsz, d = c.shape
    n = w.shape[1]
    bn = 1024
    return pl.pallas_call(
        _mod_kernel,
        grid=(n // bn,),
        in_specs=[pl.BlockSpec((bsz, d), lambda j: (0, 0)),
                  pl.BlockSpec((d, bn), lambda j: (0, j)),
                  pl.BlockSpec((1, bn), lambda j: (0, j))],
        out_specs=pl.BlockSpec((bsz, bn), lambda j: (0, j)),
        out_shape=jax.ShapeDtypeStruct((bsz, n), F32),
        compiler_params=_cparams("parallel"),
        name="mod_vectors",
    )(c, w, b.reshape(1, n))


def _rope_kernel(pos_ref, inv_ref, cos_ref, sinm_ref, sinp_ref):
    ang = pos_ref[...].astype(F32) * inv_ref[...]
    lane = lax.broadcasted_iota(jnp.int32, ang.shape, 1) % SWA_HD
    lo = lane < ROT_DIM // 2
    hi = (lane >= ROT_DIM // 2) & (lane < ROT_DIM)
    cs, sn = jnp.cos(ang), jnp.sin(ang)
    cos_ref[...] = jnp.where(lo | hi, cs, 1.0)
    sinm_ref[...] = jnp.where(lo, -sn, 0.0)
    sinp_ref[...] = jnp.where(hi, sn, 0.0)


def _rope_tables(positions):
    t = positions.size
    half = ROT_DIM // 2
    inv = ROPE_THETA ** (-jnp.arange(0, ROT_DIM, 2, dtype=F32) / ROT_DIM)
    inv_row = jnp.tile(inv, LANES // half).reshape(1, LANES)
    tb = TB_PROJ
    spec = pl.BlockSpec((tb, LANES), lambda i: (i, 0))
    return pl.pallas_call(
        _rope_kernel,
        grid=(t // tb,),
        in_specs=[pl.BlockSpec((tb, 1), lambda i: (i, 0)), pl.BlockSpec((1, LANES), lambda i: (0, 0))],
        out_specs=[spec, spec, spec],
        out_shape=[jax.ShapeDtypeStruct((t, LANES), F32)] * 3,
        compiler_params=_cparams("parallel"),
        name="rope_tables",
    )(positions.reshape(t, 1), inv_row)


def _apply_rope(x, cos, sinm, sinp):
    outs = []
    for g in range(x.shape[1] // LANES):
        xs = x[:, g * LANES:(g + 1) * LANES]
        outs.append(xs * cos + pltpu.roll(xs, LANES - ROT_DIM // 2, 1) * sinm
                    + pltpu.roll(xs, ROT_DIM // 2, 1) * sinp)
    return jnp.concatenate(outs, axis=1)


def _gla_in_kernel(x_ref, g_ref, sh_ref, sc_ref, wm_ref, wgl_ref, wg2_ref, bg2_ref, qkvo_ref, la_ref):
    h = _rms_modulate(x_ref[...], g_ref[...], sh_ref[0], sc_ref[0]).astype(MXU_DTYPE)
    qkvo_ref[...] = jnp.dot(h, wm_ref[...], preferred_element_type=F32).astype(qkvo_ref.dtype)
    gl = jnp.dot(h, wgl_ref[...], preferred_element_type=F32)
    z = jnp.dot(gl.astype(MXU_DTYPE), wg2_ref[...], preferred_element_type=F32) + bg2_ref[...]
    la_ref[...] = (jnp.minimum(z, 0.0) - jnp.log1p(jnp.exp(-jnp.abs(z)))) * (1.0 / GLA_TAU)


def _gla_in(x2d, seq, g, mod, w_in, w_g2, b_g2):
    t, d = x2d.shape
    tb = TB_PROJ
    spb = seq // tb
    n_main = GLA_HEADS * (2 * GLA_DK + 2 * GLA_DV)
    n_gate = GLA_HEADS * GLA_DK
    w_main = w_in[:, :n_main].astype(MXU_DTYPE)
    w_gl = jnp.pad(w_in[:, n_main:], ((0, 0), (0, LANES - GLA_RANK))).astype(MXU_DTYPE)
    w_g2p = jnp.pad(w_g2, ((0, LANES - GLA_RANK), (0, 0))).astype(MXU_DTYPE)
    const = lambda i: (0, 0)
    return pl.pallas_call(
        _gla_in_kernel,
        grid=(t // tb,),
        in_specs=[pl.BlockSpec((tb, d), lambda i: (i, 0)),
                  pl.BlockSpec((1, d), const),
                  pl.BlockSpec((1, 1, d), lambda i: (i // spb, 0, 0)),
                  pl.BlockSpec((1, 1, d), lambda i: (i // spb, 0, 1)),
                  pl.BlockSpec((d, n_main), const),
                  pl.BlockSpec((d, LANES), const),
                  pl.BlockSpec((LANES, n_gate), const),
                  pl.BlockSpec((1, n_gate), const)],
        out_specs=[pl.BlockSpec((tb, n_main), lambda i: (i, 0)),
                   pl.BlockSpec((tb, n_gate), lambda i: (i, 0))],
        out_shape=[jax.ShapeDtypeStruct((t, n_main), MXU_DTYPE),
                   jax.ShapeDtypeStruct((t, n_gate), F32)],
        compiler_params=_cparams("parallel"),
        name="gla_in_proj",
    )(x2d, g.reshape(1, d), mod, mod, w_main, w_gl, w_g2p, b_g2.reshape(1, n_gate))


def _gla_core_kernel(qkvo_ref, la_ref, x_ref, wout_ref, gate_ref, ng_ref, o_ref, st_ref, gated_ref):
    @pl.when(pl.program_id(1) == 0)
    def _():
        st_ref[...] = jnp.zeros_like(st_ref)

    c_len, dk, dv = GLA_CHUNK, GLA_DK, GLA_DV
    k0, v0, g0 = GLA_HEADS * dk, 2 * GLA_HEADS * dk, 2 * GLA_HEADS * dk + GLA_HEADS * dv
    row = lax.broadcasted_iota(jnp.int32, (c_len, c_len), 0)
    col = lax.broadcasted_iota(jnp.int32, (c_len, c_len), 1)
    tri = row >= col
    tri_f = tri.astype(F32)
    for c in range(x_ref.shape[0] // c_len):
        rs = slice(c * c_len, (c + 1) * c_len)
        for h in range(GLA_HEADS):
            la = la_ref[rs, h * dk:(h + 1) * dk]
            b = jnp.dot(tri_f, la, preferred_element_type=F32, precision=lax.Precision.HIGHEST)
            b_last = b[c_len - 1:c_len, :]
            q = qkvo_ref[rs, h * dk:(h + 1) * dk].astype(F32) * (dk ** -0.5)
            k = qkvo_ref[rs, k0 + h * dk:k0 + (h + 1) * dk].astype(F32)
            v = qkvo_ref[rs, v0 + h * dv:v0 + (h + 1) * dv]
            og = qkvo_ref[rs, g0 + h * dv:g0 + (h + 1) * dv].astype(F32)
            q_t = q * jnp.exp(b)
            k_t = k * jnp.exp(-b)
            k_dec = k * jnp.exp(b_last - b)
            dec = jnp.exp(b_last)
            scores = jnp.where(tri, _dot_nt(q_t, k_t), 0.0)
            st = st_ref[h]
            o = _dot(scores, v) + _dot_nt(q_t, st)
            st_ref[h] = dec * st + lax.dot_general(v.astype(MXU_DTYPE), k_dec.astype(MXU_DTYPE), _TN,
                                                   preferred_element_type=F32)
            o = o * lax.rsqrt(jnp.mean(o * o, axis=-1, keepdims=True) + EPS) * ng_ref[...]
            o = o * (og * jax.nn.sigmoid(og))
            gated_ref[rs, h * dv:(h + 1) * dv] = o.astype(gated_ref.dtype)
    y = jnp.dot(gated_ref[...], wout_ref[...], preferred_element_type=F32)
    o_ref[...] = x_ref[...] + gate_ref[0] * y


def _gla_core(x2d, bsz, seq, qkvo, la, w_out, mod, norm_g):
    t, d = x2d.shape
    tb = TB_GLA
    spb = seq // tb
    n_main, n_gate = qkvo.shape[1], la.shape[1]
    hv = GLA_HEADS * GLA_DV
    tok = lambda b, s: (b * spb + s, 0)
    return pl.pallas_call(
        _gla_core_kernel,
        grid=(bsz, spb),
        in_specs=[pl.BlockSpec((tb, n_main), tok),
                  pl.BlockSpec((tb, n_gate), tok),
                  pl.BlockSpec((tb, d), tok),
                  pl.BlockSpec((hv, d), lambda b, s: (0, 0)),
                  pl.BlockSpec((1, 1, d), lambda b, s: (b, 0, 2)),
                  pl.BlockSpec((1, GLA_DV), lambda b, s: (0, 0))],
        out_specs=pl.BlockSpec((tb, d), tok),
        out_shape=jax.ShapeDtypeStruct((t, d), F32),
        scratch_shapes=[pltpu.VMEM((GLA_HEADS, GLA_DV, GLA_DK), F32),
                        pltpu.VMEM((tb, hv), MXU_DTYPE)],
        compiler_params=_cparams("parallel", "arbitrary"),
        name="gla_core",
    )(qkvo, la, x2d, w_out.astype(MXU_DTYPE), mod, norm_g.reshape(1, GLA_DV))


def _top_values(s, k):
    vals, cnts = [], []
    for _ in range(k):
        m = jnp.max(s, axis=0, keepdims=True)
        eq = s == m
        vals.append(m)
        cnts.append(jnp.sum(jnp.where(eq, 1.0, 0.0), axis=0, keepdims=True))
        s = jnp.where(eq, NEG_INF, s)
    return jnp.concatenate(vals, axis=0), jnp.concatenate(cnts, axis=0)


def _peer_route_kernel(x_ref, g_ref, sh_ref, sc_ref, wqt_ref, keys_ref,
                       h_ref, thr_ref, coef_ref, s2_ref, e2_ref):
    kk = PEER_TOPK
    h = _rms_modulate(x_ref[...], g_ref[...], sh_ref[0], sc_ref[0]).astype(MXU_DTYPE)
    h_ref[...] = h
    q_t = lax.dot_general(wqt_ref[...], h, _NT, preferred_element_type=F32)
    half = PEER_QDIM // 2
    for hh in range(PEER_HEADS):
        r0 = hh * PEER_QDIM
        s1 = _dot(keys_ref[0], q_t[r0:r0 + half])
        s2 = _dot(keys_ref[1], q_t[r0 + half:r0 + 2 * half])
        v1, c1 = _top_values(s1, kk)
        v2, c2 = _top_values(s2, kk)
        cand = jnp.concatenate([v2 + v1[a:a + 1] for a in range(kk)], axis=0)
        wts = jnp.concatenate([c2 * c1[a:a + 1] for a in range(kk)], axis=0)
        cnt = jnp.zeros_like(v1[0:1])
        tau = jnp.full_like(cnt, NEG_INF)
        rem = cand
        for _ in range(kk):
            m = jnp.max(rem, axis=0, keepdims=True)
            eq = rem == m
            tau = jnp.where(cnt < kk, m, tau)
            cnt = cnt + jnp.sum(jnp.where(eq, wts, 0.0), axis=0, keepdims=True)
            rem = jnp.where(eq, NEG_INF, rem)
        sel = cand >= tau
        e1 = jnp.exp(v1 - v1[0:1])
        e2 = jnp.exp(v2 - v2[0:1])
        ew = jnp.concatenate([e2 * e1[a:a + 1] for a in range(kk)], axis=0)
        z = jnp.sum(jnp.where(sel, wts * ew, 0.0), axis=0, keepdims=True)
        thr = jnp.full_like(s1, POS_INF)
        for a in range(kk):
            thr_a = jnp.min(jnp.where(sel[a * kk:(a + 1) * kk], v2, POS_INF), axis=0, keepdims=True)
            thr = jnp.where(s1 == v1[a:a + 1], thr_a, thr)
        coef = jnp.exp(s1 - v1[0:1]) / z
        e2_all = jnp.exp(s2 - v2[0:1])
        for lt in range(thr_ref.shape[0]):
            ls = slice(lt * LANES, (lt + 1) * LANES)
            thr_ref[lt, hh] = thr[:, ls]
            coef_ref[lt, hh] = coef[:, ls]
            s2_ref[lt, hh] = s2[:, ls]
            e2_ref[lt, hh] = e2_all[:, ls]


def _peer_route(x2d, seq, g, mod, w_q, subkeys):
    t, d = x2d.shape
    tb = TB_PEER
    spb = seq // tb
    nq = w_q.shape[1]
    wqt = w_q.T.astype(MXU_DTYPE)
    keys = subkeys.astype(MXU_DTYPE)
    const2 = lambda i: (0, 0)
    side = pl.BlockSpec((tb // LANES, PEER_HEADS, PEER_KEYS, LANES), lambda i: (i, 0, 0, 0))
    side_shape = jax.ShapeDtypeStruct((t // LANES, PEER_HEADS, PEER_KEYS, LANES), F32)
    return pl.pallas_call(
        _peer_route_kernel,
        grid=(t // tb,),
        in_specs=[pl.BlockSpec((tb, d), lambda i: (i, 0)),
                  pl.BlockSpec((1, d), const2),
                  pl.BlockSpec((1, 1, d), lambda i: (i // spb, 0, 3)),
                  pl.BlockSpec((1, 1, d), lambda i: (i // spb, 0, 4)),
                  pl.BlockSpec((nq, d), const2),
                  pl.BlockSpec((2, PEER_KEYS, PEER_QDIM // 2), lambda i: (0, 0, 0))],
        out_specs=[pl.BlockSpec((tb, d), lambda i: (i, 0)), side, side, side, side],
        out_shape=[jax.ShapeDtypeStruct((t, d), MXU_DTYPE), side_shape, side_shape, side_shape, side_shape],
        compiler_params=_cparams("parallel"),
        name="peer_route",
    )(x2d, g.reshape(1, d), mod, mod, wqt, keys)


def _gelu_exact(x):
    return 0.5 * x * (1.0 + lax.erf(x * 0.7071067811865476))


def _peer_dense_kernel(h_ref, u_ref, vt_ref, thr_ref, coef_ref, s2_ref, e2_ref, x_ref, gate_ref,
                       o_ref, st_ref, w_ref, acc_ref):
    c = pl.program_id(1)
    ec, tb = st_ref.shape
    groups = ec // PEER_KEYS

    @pl.when(c == 0)
    def _():
        acc_ref[...] = jnp.zeros_like(acc_ref)

    st_ref[...] = lax.dot_general(u_ref[...], h_ref[...], _NT, preferred_element_type=F32)

    def group_body(j, carry):
        i1 = c * groups + j
        r0 = pl.multiple_of(j * PEER_KEYS, PEER_KEYS)
        for lt in range(tb // LANES):
            ls = slice(lt * LANES, (lt + 1) * LANES)
            gw = jnp.zeros((PEER_KEYS, LANES), F32)
            for hh in range(PEER_HEADS):
                thr = thr_ref[lt, hh, pl.ds(i1, 8, stride=0), :]
                cf = coef_ref[lt, hh, pl.ds(i1, 8, stride=0), :]
                thr = jnp.concatenate([thr] * (PEER_KEYS // 8), axis=0)
                cf = jnp.concatenate([cf] * (PEER_KEYS // 8), axis=0)
                gw = gw + jnp.where(s2_ref[lt, hh] >= thr, e2_ref[lt, hh] * cf, 0.0)
            a = _gelu_exact(st_ref[pl.ds(r0, PEER_KEYS), ls])
            w_ref[pl.ds(r0, PEER_KEYS), ls] = (a * gw).astype(w_ref.dtype)
        return carry

    lax.fori_loop(0, groups, group_body, 0)
    acc_ref[...] += jnp.dot(vt_ref[...], w_ref[...], preferred_element_type=F32)

    @pl.when(c == pl.num_programs(1) - 1)
    def _():
        o_ref[...] = x_ref[...] + gate_ref[0] * acc_ref[...].T


def _peer_dense(x2d, seq, h, side, mod, u_tab, v_tab):
    t, d = x2d.shape
    tb, ec = TB_PEER, EC_PEER
    spb = seq // tb
    n_exp = u_tab.shape[0]
    u = u_tab.astype(MXU_DTYPE)
    vt = v_tab.T.astype(MXU_DTYPE)
    thr, coef, s2, e2 = side
    side_spec = pl.BlockSpec((tb // LANES, PEER_HEADS, PEER_KEYS, LANES), lambda i, c: (i, 0, 0, 0))
    return pl.pallas_call(
        _peer_dense_kernel,
        grid=(t // tb, n_exp // ec),
        in_specs=[pl.BlockSpec((tb, d), lambda i, c: (i, 0)),
                  pl.BlockSpec((ec, d), lambda i, c: (c, 0)),
                  pl.BlockSpec((d, ec), lambda i, c: (0, c)),
                  side_spec, side_spec, side_spec, side_spec,
                  pl.BlockSpec((tb, d), lambda i, c: (i, 0)),
                  pl.BlockSpec((1, 1, d), lambda i, c: (i // spb, 0, 5))],
        out_specs=pl.BlockSpec((tb, d), lambda i, c: (i, 0)),
        out_shape=jax.ShapeDtypeStruct((t, d), F32),
        scratch_shapes=[pltpu.VMEM((ec, tb), F32),
                        pltpu.VMEM((ec, tb), MXU_DTYPE),
                        pltpu.VMEM((d, tb), F32)],
        compiler_params=_cparams("parallel", "arbitrary"),
        name="peer_dense",
    )(h, u, vt, thr, coef, s2, e2, x2d, mod)


def _peer(x2d, seq, g, mod, w_q, subkeys, u_tab, v_tab):
    h, *side = _peer_route(x2d, seq, g, mod, w_q, subkeys)
    return _peer_dense(x2d, seq, h, side, mod, u_tab, v_tab)


def _kvq_kernel(x_ref, gkv_ref, shkv_ref, sckv_ref, gq_ref, shq_ref, scq_ref, wkv_ref, wq_ref,
                cos_ref, sinm_ref, sinp_ref, k_ref, v_ref, q_ref):
    x = x_ref[...]
    cos, sinm, sinp = cos_ref[...], sinm_ref[...], sinp_ref[...]
    hk = _rms_modulate(x, gkv_ref[...], shkv_ref[0], sckv_ref[0])
    kv = _dot(hk, wkv_ref[...])
    nk = k_ref.shape[1]
    k_ref[...] = _apply_rope(kv[:, :nk], cos, sinm, sinp).astype(k_ref.dtype)
    v_ref[...] = kv[:, nk:].astype(v_ref.dtype)
    hq = _rms_modulate(x, gq_ref[...], shq_ref[0], scq_ref[0])
    q_ref[...] = _apply_rope(_dot(hq, wq_ref[...]), cos, sinm, sinp).astype(q_ref.dtype)


def _kvq(x2d, seq, kv_norm_g, kv_mod, kv_w, q_norm_g, mod, w_q, rope):
    t, d = x2d.shape
    tb = TB_PROJ
    spb = seq // tb
    nk = SWA_KV_HEADS * SWA_HD
    nq = SWA_HEADS * SWA_HD
    const = lambda i: (0, 0)
    tok = lambda w: pl.BlockSpec((tb, w), lambda i: (i, 0))
    modspec = lambda k: pl.BlockSpec((1, 1, d), lambda i: (i // spb, 0, k))
    return pl.pallas_call(
        _kvq_kernel,
        grid=(t // tb,),
        in_specs=[tok(d),
                  pl.BlockSpec((1, d), const), modspec(0), modspec(1),
                  pl.BlockSpec((1, d), const), modspec(0), modspec(1),
                  pl.BlockSpec((d, 2 * nk), const), pl.BlockSpec((d, nq), const),
                  tok(LANES), tok(LANES), tok(LANES)],
        out_specs=[tok(nk), tok(nk), tok(nq)],
        out_shape=[jax.ShapeDtypeStruct((t, nk), MXU_DTYPE), jax.ShapeDtypeStruct((t, nk), MXU_DTYPE),
                   jax.ShapeDtypeStruct((t, nq), MXU_DTYPE)],
        compiler_params=_cparams("parallel"),
        name="kv_q_proj",
    )(x2d, kv_norm_g.reshape(1, d), kv_mod, kv_mod, q_norm_g.reshape(1, d), mod, mod,
      kv_w.astype(MXU_DTYPE), w_q.astype(MXU_DTYPE), *rope)


def _swa_kernel(sink_ref, q_ref, kc_ref, kp_ref, vc_ref, vp_ref, x_ref, wout_ref, gate_ref,
                o_ref, att_ref):
    p = SWA_WINDOW
    group = SWA_HEADS // SWA_KV_HEADS
    first = pl.program_id(1) == 0
    lane = lax.broadcasted_iota(jnp.int32, (p, LANES), 1)
    rows = 2 * p
    qi = lax.broadcasted_iota(jnp.int32, (rows, 2 * p), 0) % p
    mi = lax.broadcasted_iota(jnp.int32, (rows, 2 * p), 1)
    band = (mi > qi) & (mi <= qi + p)
    top_half = lax.broadcasted_iota(jnp.int32, (rows, 1), 0) < p
    zero = jnp.zeros((), MXU_DTYPE)
    for jb in range(q_ref.shape[0] // p):
        rs = slice(jb * p, (jb + 1) * p)
        if jb == 0:
            k_prev, v_prev = kp_ref[...], vp_ref[...]
            mask = band & ((mi >= p) | jnp.logical_not(first))
        else:
            ps = slice((jb - 1) * p, jb * p)
            k_prev, v_prev = kc_ref[ps, :], vc_ref[ps, :]
            mask = band
        k_band = jnp.concatenate([k_prev, kc_ref[rs, :]], axis=0)
        v_band = jnp.concatenate([v_prev, vc_ref[rs, :]], axis=0)
        for kh in range(SWA_KV_HEADS):
            kv_tile = k_band[:, (kh // 2) * LANES:(kh // 2 + 1) * LANES]
            vv_tile = v_band[:, (kh // 2) * LANES:(kh // 2 + 1) * LANES]
            mine = (lane[0:1] // SWA_HD) == (kh % 2)
            k_own = jnp.where(mine, kv_tile, zero)
            v_own = jnp.where(mine, vv_tile, zero)
            k_swp = pltpu.roll(k_own.astype(F32), SWA_HD, 1).astype(MXU_DTYPE)
            v_swp = pltpu.roll(v_own.astype(F32), SWA_HD, 1).astype(MXU_DTYPE)
            k_lo, k_hi = (k_own, k_swp) if kh % 2 == 0 else (k_swp, k_own)
            v_lo, v_hi = (v_own, v_swp) if kh % 2 == 0 else (v_swp, v_own)
            t0 = kh * group // 2
            q2 = jnp.concatenate([q_ref[rs, t0 * LANES:(t0 + 1) * LANES],
                                  q_ref[rs, (t0 + 1) * LANES:(t0 + 2) * LANES]], axis=0)
            out = jnp.zeros((rows, LANES), F32)
            for pos, (k_x, v_x) in enumerate(((k_lo, v_lo), (k_hi, v_hi))):
                s = _dot_nt(q2, k_x) * (SWA_HD ** -0.5)
                s = jnp.where(mask, s, NEG_INF)
                sink = jnp.where(top_half, sink_ref[kh * group + pos], sink_ref[kh * group + 2 + pos])
                m = jnp.maximum(jnp.max(s, axis=-1, keepdims=True), sink)
                e = jnp.exp(s - m)
                probs = e / (jnp.sum(e, axis=-1, keepdims=True) + jnp.exp(sink - m))
                out = out + _dot(probs, v_x)
            att_ref[rs, t0 * LANES:(t0 + 1) * LANES] = out[:p].astype(att_ref.dtype)
            att_ref[rs, (t0 + 1) * LANES:(t0 + 2) * LANES] = out[p:].astype(att_ref.dtype)
    y = jnp.dot(att_ref[...], wout_ref[...], preferred_element_type=F32)
    o_ref[...] = x_ref[...] + gate_ref[0] * y


def _swa(x2d, bsz, seq, q, k, v, sinks, w_out, mod):
    t, d = x2d.shape
    tb, p = TB_SWA, SWA_WINDOW
    spb = seq // tb
    nk, nq = k.shape[1], q.shape[1]
    tok = lambda b, s: (b * spb + s, 0)
    prev = lambda b, s: (jnp.maximum((b * spb + s) * (tb // p) - 1, 0), 0)
    return pl.pallas_call(
        _swa_kernel,
        grid=(bsz, spb),
        in_specs=[pl.BlockSpec(memory_space=pltpu.SMEM),
                  pl.BlockSpec((tb, nq), tok),
                  pl.BlockSpec((tb, nk), tok), pl.BlockSpec((p, nk), prev),
                  pl.BlockSpec((tb, nk), tok), pl.BlockSpec((p, nk), prev),
                  pl.BlockSpec((tb, d), tok),
                  pl.BlockSpec((nq, d), lambda b, s: (0, 0)),
                  pl.BlockSpec((1, 1, d), lambda b, s: (b, 0, 2))],
        out_specs=pl.BlockSpec((tb, d), tok),
        out_shape=jax.ShapeDtypeStruct((t, d), F32),
        scratch_shapes=[pltpu.VMEM((tb, nq), MXU_DTYPE)],
        compiler_params=_cparams("parallel", "parallel"),
        name="swa_attention",
    )(sinks, q, k, k, v, v, x2d, w_out.astype(MXU_DTYPE), mod)


def _final_norm_kernel(x_ref, g_ref, o_ref):
    x = x_ref[...]
    o_ref[...] = x * lax.rsqrt(jnp.mean(x * x, axis=-1, keepdims=True) + EPS) * g_ref[...]


def _final_norm(x2d, g):
    t, d = x2d.shape
    tb = TB_PROJ
    return pl.pallas_call(
        _final_norm_kernel,
        grid=(t // tb,),
        in_specs=[pl.BlockSpec((tb, d), lambda i: (i, 0)), pl.BlockSpec((1, d), lambda i: (0, 0))],
        out_specs=pl.BlockSpec((tb, d), lambda i: (i, 0)),
        out_shape=jax.ShapeDtypeStruct((t, d), F32),
        compiler_params=_cparams("parallel"),
        name="final_norm",
    )(x2d, g.reshape(1, d))


def kernel(x, c, positions, mod_w, mod_b, norm_g, gla_w_in, gla_w_g2, gla_b_g2, gla_norm_g, gla_w_out,
           kv_mod_w, kv_mod_b, kv_norm_g, kv_w, swa_w_q, swa_sinks, swa_w_out, peer_w_q, peer_subkeys,
           peer_u, peer_v, final_norm_g):
    bsz, seq, d = x.shape
    x2d = x.reshape(bsz * seq, d)
    mod0 = _mod_vectors(c, mod_w[0], mod_b[0]).reshape(bsz, 1, 6 * d)
    mod1 = _mod_vectors(c, mod_w[1], mod_b[1]).reshape(bsz, 1, 6 * d)
    kv_mod = _mod_vectors(c, kv_mod_w, kv_mod_b).reshape(bsz, 1, 2 * d)
    rope = _rope_tables(positions)

    qkvo, la = _gla_in(x2d, seq, norm_g[0, 0], mod0, gla_w_in[0], gla_w_g2[0], gla_b_g2[0])
    x2d = _gla_core(x2d, bsz, seq, qkvo, la, gla_w_out[0], mod0, gla_norm_g[0])
    x2d = _peer(x2d, seq, norm_g[0, 1], mod0, peer_w_q[0], peer_subkeys[0], peer_u[0], peer_v[0])

    k, v, q = _kvq(x2d, seq, kv_norm_g, kv_mod, kv_w, norm_g[1, 0], mod1, swa_w_q[0], rope)
    x2d = _swa(x2d, bsz, seq, q, k, v, swa_sinks[0], swa_w_out[0], mod1)
    x2d = _peer(x2d, seq, norm_g[1, 1], mod1, peer_w_q[1], peer_subkeys[1], peer_u[1], peer_v[1])

    return _final_norm(x2d, final_norm_g).reshape(bsz, seq, d)
```

```python
import functools

import jax
import jax.numpy as jnp
from jax import lax
from jax.experimental import pallas as pl
from jax.experimental.pallas import tpu as pltpu

F32 = jnp.float32
MXU_DTYPE = jnp.bfloat16
EPS = 1e-6
NEG_INF = float("-inf")
POS_INF = float("inf")

LANES = 128
VMEM_LIMIT_BYTES = 56 * 1024 * 1024

GLA_HEADS, GLA_DK, GLA_DV, GLA_RANK, GLA_TAU, GLA_CHUNK = 4, 128, 256, 16, 16.0, 64
SWA_HEADS, SWA_KV_HEADS, SWA_HD, SWA_WINDOW = 16, 4, 64, 128
ROT_DIM, ROPE_THETA = 16, 500000.0
PEER_HEADS, PEER_KEYS, PEER_TOPK, PEER_QDIM = 8, 128, 16, 256

TB_PROJ = 512
TB_GLA = 256
TB_SWA = 512
TB_PEER = 512
TB_ROUTE = 256
EC_PEER = 1024

_NT = (((1,), (1,)), ((), ()))
_TN = (((0,), (0,)), ((), ()))


def _cparams(*sem):
    return pltpu.CompilerParams(dimension_semantics=sem, vmem_limit_bytes=VMEM_LIMIT_BYTES)


def _rms_modulate(x, g, shift, scale):
    y = x * lax.rsqrt(jnp.mean(x * x, axis=-1, keepdims=True) + EPS) * g
    return y * (1.0 + scale) + shift


def _dot(a, b):
    return jnp.dot(a.astype(MXU_DTYPE), b.astype(MXU_DTYPE), preferred_element_type=F32)


def _dot_nt(a, b):
    return lax.dot_general(a.astype(MXU_DTYPE), b.astype(MXU_DTYPE), _NT, preferred_element_type=F32)


def _mod_kernel(c_ref, w_ref, b_ref, o_ref):
    c = c_ref[...]
    ca = c * jax.nn.sigmoid(c)
    o_ref[...] = jnp.dot(ca, w_ref[...], preferred_element_type=F32,
                         precision=lax.Precision.HIGHEST) + b_ref[...]


def _mod_vectors(c, w, b):
    bsz, d = c.shape
    n = w.shape[1]
    bn = 1024
    return pl.pallas_call(
        _mod_kernel,
        grid=(n // bn,),
        in_specs=[pl.BlockSpec((bsz, d), lambda j: (0, 0)),
                  pl.BlockSpec((d, bn), lambda j: (0, j)),
                  pl.BlockSpec((1, bn), lambda j: (0, j))],
        out_specs=pl.BlockSpec((bsz, bn), lambda j: (0, j)),
        out_shape=jax.ShapeDtypeStruct((bsz, n), F32),
        compiler_params=_cparams("parallel"),
        name="mod_vectors",
    )(c, w, b.reshape(1, n))


def _rope_kernel(pos_ref, inv_ref, cos_ref, sinm_ref, sinp_ref):
    ang = pos_ref[...].astype(F32) * inv_ref[...]
    lane = lax.broadcasted_iota(jnp.int32, ang.shape, 1) % SWA_HD
    lo = lane < ROT_DIM // 2
    hi = (lane >= ROT_DIM // 2) & (lane < ROT_DIM)
    cs, sn = jnp.cos(ang), jnp.sin(ang)
    cos_ref[...] = jnp.where(lo | hi, cs, 1.0)
    sinm_ref[...] = jnp.where(lo, -sn, 0.0)
    sinp_ref[...] = jnp.where(hi, sn, 0.0)


def _rope_tables(positions):
    t = positions.size
    half = ROT_DIM // 2
    inv = ROPE_THETA ** (-jnp.arange(0, ROT_DIM, 2, dtype=F32) / ROT_DIM)
    inv_row = jnp.tile(inv, LANES // half).reshape(1, LANES)
    tb = TB_PROJ
    spec = pl.BlockSpec((tb, LANES), lambda i: (i, 0))
    return pl.pallas_call(
        _rope_kernel,
        grid=(t // tb,),
        in_specs=[pl.BlockSpec((tb, 1), lambda i: (i, 0)), pl.BlockSpec((1, LANES), lambda i: (0, 0))],
        out_specs=[spec, spec, spec],
        out_shape=[jax.ShapeDtypeStruct((t, LANES), F32)] * 3,
        compiler_params=_cparams("parallel"),
        name="rope_tables",
    )(positions.reshape(t, 1), inv_row)


def _apply_rope(x, cos, sinm, sinp):
    outs = []
    for g in range(x.shape[1] // LANES):
        xs = x[:, g * LANES:(g + 1) * LANES]
        outs.append(xs * cos + pltpu.roll(xs, LANES - ROT_DIM // 2, 1) * sinm
                    + pltpu.roll(xs, ROT_DIM // 2, 1) * sinp)
    return jnp.concatenate(outs, axis=1)


def _gla_in_kernel(x_ref, g_ref, sh_ref, sc_ref, wm_ref, wgl_ref, wg2_ref, bg2_ref, qkvo_ref, la_ref):
    h = _rms_modulate(x_ref[...], g_ref[...], sh_ref[0], sc_ref[0]).astype(MXU_DTYPE)
    qkvo_ref[...] = jnp.dot(h, wm_ref[...], preferred_element_type=F32).astype(qkvo_ref.dtype)
    gl = jnp.dot(h, wgl_ref[...], preferred_element_type=F32)
    z = jnp.dot(gl.astype(MXU_DTYPE), wg2_ref[...], preferred_element_type=F32) + bg2_ref[...]
    la_ref[...] = (jnp.minimum(z, 0.0) - jnp.log1p(jnp.exp(-jnp.abs(z)))) * (1.0 / GLA_TAU)


def _gla_in(x2d, seq, g, mod, w_in, w_g2, b_g2):
    t, d = x2d.shape
    tb = TB_PROJ
    spb = seq // tb
    n_main = GLA_HEADS * (2 * GLA_DK + 2 * GLA_DV)
    n_gate = GLA_HEADS * GLA_DK
    w_main = w_in[:, :n_main].astype(MXU_DTYPE)
    w_gl = jnp.pad(w_in[:, n_main:], ((0, 0), (0, LANES - GLA_RANK))).astype(MXU_DTYPE)
    w_g2p = jnp.pad(w_g2, ((0, LANES - GLA_RANK), (0, 0))).astype(MXU_DTYPE)
    const = lambda i: (0, 0)
    return pl.pallas_call(
        _gla_in_kernel,
        grid=(t // tb,),
        in_specs=[pl.BlockSpec((tb, d), lambda i: (i, 0)),
                  pl.BlockSpec((1, d), const),
                  pl.BlockSpec((1, 1, d), lambda i: (i // spb, 0, 0)),
                  pl.BlockSpec((1, 1, d), lambda i: (i // spb, 0, 1)),
                  pl.BlockSpec((d, n_main), const),
                  pl.BlockSpec((d, LANES), const),
                  pl.BlockSpec((LANES, n_gate), const),
                  pl.BlockSpec((1, n_gate), const)],
        out_specs=[pl.BlockSpec((tb, n_main), lambda i: (i, 0)),
                   pl.BlockSpec((tb, n_gate), lambda i: (i, 0))],
        out_shape=[jax.ShapeDtypeStruct((t, n_main), MXU_DTYPE),
                   jax.ShapeDtypeStruct((t, n_gate), F32)],
        compiler_params=_cparams("parallel"),
        name="gla_in_proj",
    )(x2d, g.reshape(1, d), mod, mod, w_main, w_gl, w_g2p, b_g2.reshape(1, n_gate))


def _gla_core_kernel(qkvo_ref, la_ref, x_ref, wout_ref, gate_ref, ng_ref, o_ref, st_ref, gated_ref):
    @pl.when(pl.program_id(1) == 0)
    def _():
        st_ref[...] = jnp.zeros_like(st_ref)

    c_len, dk, dv = GLA_CHUNK, GLA_DK, GLA_DV
    k0, v0, g0 = GLA_HEADS * dk, 2 * GLA_HEADS * dk, 2 * GLA_HEADS * dk + GLA_HEADS * dv
    row = lax.broadcasted_iota(jnp.int32, (c_len, c_len), 0)
    col = lax.broadcasted_iota(jnp.int32, (c_len, c_len), 1)
    tri = row >= col
    tri_f = tri.astype(F32)
    for c in range(x_ref.shape[0] // c_len):
        rs = slice(c * c_len, (c + 1) * c_len)
        for h in range(GLA_HEADS):
            la = la_ref[rs, h * dk:(h + 1) * dk]
            b = jnp.dot(tri_f, la, preferred_element_type=F32, precision=lax.Precision.HIGHEST)
            b_last = b[c_len - 1:c_len, :]
            q = qkvo_ref[rs, h * dk:(h + 1) * dk].astype(F32) * (dk ** -0.5)
            k = qkvo_ref[rs, k0 + h * dk:k0 + (h + 1) * dk].astype(F32)
            v = qkvo_ref[rs, v0 + h * dv:v0 + (h + 1) * dv]
            og = qkvo_ref[rs, g0 + h * dv:g0 + (h + 1) * dv].astype(F32)
            q_t = q * jnp.exp(b)
            k_t = k * jnp.exp(-b)
            k_dec = k * jnp.exp(b_last - b)
            dec = jnp.exp(b_last)
            scores = jnp.where(tri, _dot_nt(q_t, k_t), 0.0)
            st = st_ref[h]
            o = _dot(scores, v) + _dot_nt(q_t, st)
            st_ref[h] = dec * st + lax.dot_general(v.astype(MXU_DTYPE), k_dec.astype(MXU_DTYPE), _TN,
                                                   preferred_element_type=F32)
            o = o * lax.rsqrt(jnp.mean(o * o, axis=-1, keepdims=True) + EPS) * ng_ref[...]
            o = o * (og * jax.nn.sigmoid(og))
            gated_ref[rs, h * dv:(h + 1) * dv] = o.astype(gated_ref.dtype)
    y = jnp.dot(gated_ref[...], wout_ref[...], preferred_element_type=F32)
    o_ref[...] = x_ref[...] + gate_ref[0] * y


def _gla_core(x2d, bsz, seq, qkvo, la, w_out, mod, norm_g):
    t, d = x2d.shape
    tb = TB_GLA
    spb = seq // tb
    n_main, n_gate = qkvo.shape[1], la.shape[1]
    hv = GLA_HEADS * GLA_DV
    tok = lambda b, s: (b * spb + s, 0)
    return pl.pallas_call(
        _gla_core_kernel,
        grid=(bsz, spb),
        in_specs=[pl.BlockSpec((tb, n_main), tok),
                  pl.BlockSpec((tb, n_gate), tok),
                  pl.BlockSpec((tb, d), tok),
                  pl.BlockSpec((hv, d), lambda b, s: (0, 0)),
                  pl.BlockSpec((1, 1, d), lambda b, s: (b, 0, 2)),
                  pl.BlockSpec((1, GLA_DV), lambda b, s: (0, 0))],
        out_specs=pl.BlockSpec((tb, d), tok),
        out_shape=jax.ShapeDtypeStruct((t, d), F32),
        scratch_shapes=[pltpu.VMEM((GLA_HEADS, GLA_DV, GLA_DK), F32),
                        pltpu.VMEM((tb, hv), MXU_DTYPE)],
        compiler_params=_cparams("parallel", "arbitrary"),
        name="gla_core",
    )(qkvo, la, x2d, w_out.astype(MXU_DTYPE), mod, norm_g.reshape(1, GLA_DV))


def _sort16_pairs():
    def merge(lo, hi, r):
        step = r * 2
        if step < hi - lo:
            yield from merge(lo, hi, step)
            yield from merge(lo + r, hi, step)
            yield from [(i, i + r) for i in range(lo + r, hi - r, step)]
        else:
            yield (lo, lo + r)

    def sort(lo, hi):
        if hi - lo >= 1:
            mid = lo + (hi - lo) // 2
            yield from sort(lo, mid)
            yield from sort(mid + 1, hi)
            yield from merge(lo, hi, 1)

    return tuple(sort(0, PEER_TOPK - 1))


_SORT16 = _sort16_pairs()


def _sort16_desc(w):
    w = list(w)
    for i, j in _SORT16:
        w[i], w[j] = jnp.maximum(w[i], w[j]), jnp.minimum(w[i], w[j])
    return w


def _merge_top16(a, b):
    n = PEER_TOPK
    w = [jnp.maximum(a[i], b[n - 1 - i]) for i in range(n)]
    d = n // 2
    while d >= 1:
        for i in range(n):
            if i & d == 0:
                w[i], w[i + d] = jnp.maximum(w[i], w[i + d]), jnp.minimum(w[i], w[i + d])
        d //= 2
    return w


def _top16_desc(wires):
    n = PEER_TOPK
    runs = [_sort16_desc(wires[i:i + n]) for i in range(0, len(wires), n)]
    while len(runs) > 1:
        runs = [_merge_top16(runs[i], runs[i + 1]) for i in range(0, len(runs), 2)]
    return runs[0]


def _route_tile(s1, s2):
    n = PEER_TOPK
    v1, v2 = _top16_desc(s1), _top16_desc(s2)
    run = [v1[0] + v2[b] for b in range(n)]
    for a in range(1, n):
        run = _merge_top16(run, [v1[a] + v2[b] for b in range(n)])
    tau = run[n - 1]
    e1 = [jnp.exp(v - v1[0]) for v in v1]
    e2 = [jnp.exp(v - v2[0]) for v in v2]
    z = jnp.zeros_like(tau)
    thr_rank = []
    for a in range(n):
        za = jnp.zeros_like(tau)
        ta = jnp.full_like(tau, POS_INF)
        for b in range(n):
            sel = (v1[a] + v2[b]) >= tau
            za = za + jnp.where(sel, e2[b], 0.0)
            ta = jnp.where(sel, v2[b], ta)
        z = z + e1[a] * za
        thr_rank.append(ta)
    rz = 1.0 / z
    thr, coef = [], []
    for s in s1:
        t = jnp.full_like(tau, POS_INF)
        for a in range(n - 1, -1, -1):
            t = jnp.where(s == v1[a], thr_rank[a], t)
        thr.append(t)
        coef.append(jnp.exp(s - v1[0]) * rz)
    return thr, coef, v2[0]


def _peer_route_kernel(x_ref, g_ref, sh_ref, sc_ref, wqt_ref, k1i_ref, k2i_ref, k2n_ref,
                       h_ref, thr_ref, coef_ref, s2_ref, e2_ref):
    nk, nh = PEER_KEYS, PEER_HEADS
    h = _rms_modulate(x_ref[...], g_ref[...], sh_ref[0], sc_ref[0]).astype(MXU_DTYPE)
    h_ref[...] = h
    q_t = lax.dot_general(wqt_ref[...], h, _NT, preferred_element_type=F32).astype(MXU_DTYPE)
    half = q_t.shape[0] // 2
    s1i = jnp.dot(k1i_ref[...], q_t[:half], preferred_element_type=F32)
    s2i = jnp.dot(k2i_ref[...], q_t[half:], preferred_element_type=F32)
    s2n = jnp.dot(k2n_ref[...], q_t[half:], preferred_element_type=F32)
    for lt in range(thr_ref.shape[0]):
        ls = slice(lt * LANES, (lt + 1) * LANES)
        w1 = [s1i[k * nh:(k + 1) * nh, ls] for k in range(nk)]
        w2 = [s2i[k * nh:(k + 1) * nh, ls] for k in range(nk)]
        thr, coef, m2 = _route_tile(w1, w2)
        for k in range(nk):
            thr_ref[lt, k] = thr[k]
            coef_ref[lt, k] = coef[k]
        for hh in range(nh):
            s2h = s2n[hh * nk:(hh + 1) * nk, ls]
            s2_ref[lt, hh] = s2h
            e2_ref[lt, hh] = jnp.exp(s2h - m2[hh:hh + 1])


def _peer_route(x2d, seq, g, mod, w_q, subkeys):
    t, d = x2d.shape
    tb = TB_ROUTE
    spb = seq // tb
    nh, nk, half = PEER_HEADS, PEER_KEYS, PEER_QDIM // 2
    nq = nh * half
    wqt = w_q.reshape(d, nh, 2, half).transpose(2, 1, 3, 0).reshape(2 * nq, d).astype(MXU_DTYPE)
    eye = jnp.eye(nh, dtype=subkeys.dtype)
    interleaved = lambda k: (k[:, None, None, :] * eye[None, :, :, None]).reshape(nk * nh, nq).astype(MXU_DTYPE)
    natural = lambda k: (eye[:, None, :, None] * k[None, :, None, :]).reshape(nh * nk, nq).astype(MXU_DTYPE)
    const2 = lambda i: (0, 0)
    tiles = tb // LANES
    side_k = pl.BlockSpec((tiles, nk, nh, LANES), lambda i: (i, 0, 0, 0))
    side_h = pl.BlockSpec((tiles, nh, nk, LANES), lambda i: (i, 0, 0, 0))
    shape_k = jax.ShapeDtypeStruct((t // LANES, nk, nh, LANES), F32)
    shape_h = jax.ShapeDtypeStruct((t // LANES, nh, nk, LANES), F32)
    return pl.pallas_call(
        _peer_route_kernel,
        grid=(t // tb,),
        in_specs=[pl.BlockSpec((tb, d), lambda i: (i, 0)),
                  pl.BlockSpec((1, d), const2),
                  pl.BlockSpec((1, 1, d), lambda i: (i // spb, 0, 3)),
                  pl.BlockSpec((1, 1, d), lambda i: (i // spb, 0, 4)),
                  pl.BlockSpec((2 * nq, d), const2),
                  pl.BlockSpec((nk * nh, nq), const2),
                  pl.BlockSpec((nk * nh, nq), const2),
                  pl.BlockSpec((nh * nk, nq), const2)],
        out_specs=[pl.BlockSpec((tb, d), lambda i: (i, 0)), side_k, side_k, side_h, side_h],
        out_shape=[jax.ShapeDtypeStruct((t, d), MXU_DTYPE), shape_k, shape_k, shape_h, shape_h],
        compiler_params=_cparams("parallel"),
        name="peer_route",
    )(x2d, g.reshape(1, d), mod, mod, wqt, interleaved(subkeys[0]), interleaved(subkeys[1]),
      natural(subkeys[1]))


def _gelu_exact(x):
    return 0.5 * x * (1.0 + lax.erf(x * 0.7071067811865476))


def _dense_weights(s_ref, w_ref, thr_ref, coef_ref, s2_ref, e2_ref, chunk):
    ec, tb = s_ref.shape
    groups = ec // PEER_KEYS
    for j in range(groups):
        i1 = chunk * groups + j
        rs = slice(j * PEER_KEYS, (j + 1) * PEER_KEYS)
        for lt in range(tb // LANES):
            ls = slice(lt * LANES, (lt + 1) * LANES)
            thr = thr_ref[lt, i1]
            cf = coef_ref[lt, i1]
            gw = jnp.zeros((PEER_KEYS, LANES), F32)
            for hh in range(PEER_HEADS):
                gw = gw + jnp.where(s2_ref[lt, hh] >= thr[hh:hh + 1], e2_ref[lt, hh] * cf[hh:hh + 1], 0.0)
            w_ref[rs, ls] = (_gelu_exact(s_ref[rs, ls]) * gw).astype(w_ref.dtype)


def _peer_dense_kernel(hc_ref, hn_ref, h0_ref, uo_ref, un_ref, u0_ref, vt_ref,
                       thr_ref, coef_ref, s2_ref, e2_ref, x_ref, gate_ref,
                       o_ref, sa_ref, sb_ref, wa_ref, wb_ref, acc_ref, *, steps):
    n = pl.program_id(0)
    m = n % steps
    ec = sa_ref.shape[0]

    @pl.when(n == 0)
    def _():
        sa_ref[...] = lax.dot_general(u0_ref[...], h0_ref[...], _NT, preferred_element_type=F32)
        acc_ref[...] = jnp.zeros_like(acc_ref)

    sb_ref[...] = lax.dot_general(uo_ref[...], hc_ref[...], _NT, preferred_element_type=F32)
    _dense_weights(sa_ref, wa_ref, thr_ref, coef_ref, s2_ref, e2_ref, 2 * m)
    acc_ref[...] += jnp.dot(vt_ref[:, :ec], wa_ref[...], preferred_element_type=F32)

    sa_ref[...] = lax.dot_general(un_ref[...], hn_ref[...], _NT, preferred_element_type=F32)
    _dense_weights(sb_ref, wb_ref, thr_ref, coef_ref, s2_ref, e2_ref, 2 * m + 1)
    acc_ref[...] += jnp.dot(vt_ref[:, ec:], wb_ref[...], preferred_element_type=F32)

    @pl.when(m == steps - 1)
    def _():
        o_ref[...] = x_ref[...] + gate_ref[0] * acc_ref[...].T
        acc_ref[...] = jnp.zeros_like(acc_ref)


def _peer_dense(x2d, seq, h, side, mod, u_tab, v_tab):
    t, d = x2d.shape
    tb, ec = TB_PEER, EC_PEER
    spb = seq // tb
    n_exp = u_tab.shape[0]
    steps = n_exp // (2 * ec)
    n_steps = (t // tb) * steps
    u = u_tab.astype(MXU_DTYPE)
    vt = v_tab.T.astype(MXU_DTYPE)
    thr, coef, s2, e2 = side
    tiles = tb // LANES
    once = pl.Buffered(1)
    blk = lambda n: n // steps
    nxt = lambda n: jnp.minimum(n + 1, n_steps - 1)
    side_k = pl.BlockSpec((tiles, PEER_KEYS, PEER_HEADS, LANES), lambda n: (blk(n), 0, 0, 0), pipeline_mode=once)
    side_h = pl.BlockSpec((tiles, PEER_HEADS, PEER_KEYS, LANES), lambda n: (blk(n), 0, 0, 0), pipeline_mode=once)
    return pl.pallas_call(
        functools.partial(_peer_dense_kernel, steps=steps),
        grid=(n_steps,),
        in_specs=[pl.BlockSpec((tb, d), lambda n: (blk(n), 0)),
                  pl.BlockSpec((tb, d), lambda n: (blk(nxt(n)), 0)),
                  pl.BlockSpec((tb, d), lambda n: (0, 0), pipeline_mode=once),
                  pl.BlockSpec((ec, d), lambda n: (2 * (n % steps) + 1, 0)),
                  pl.BlockSpec((ec, d), lambda n: (2 * (nxt(n) % steps), 0)),
                  pl.BlockSpec((ec, d), lambda n: (0, 0), pipeline_mode=once),
                  pl.BlockSpec((d, 2 * ec), lambda n: (0, n % steps)),
                  side_k, side_k, side_h, side_h,
                  pl.BlockSpec((tb, d), lambda n: (blk(n), 0)),
                  pl.BlockSpec((1, 1, d), lambda n: (blk(n) // spb, 0, 5))],
        out_specs=pl.BlockSpec((tb, d), lambda n: (blk(n), 0)),
        out_shape=jax.ShapeDtypeStruct((t, d), F32),
        scratch_shapes=[pltpu.VMEM((ec, tb), F32), pltpu.VMEM((ec, tb), F32),
                        pltpu.VMEM((ec, tb), MXU_DTYPE), pltpu.VMEM((ec, tb), MXU_DTYPE),
                        pltpu.VMEM((d, tb), F32)],
        compiler_params=_cparams("arbitrary"),
        name="peer_dense",
    )(h, h, h, u, u, u, vt, thr, coef, s2, e2, x2d, mod)


def _peer(x2d, seq, g, mod, w_q, subkeys, u_tab, v_tab):
    h, *side = _peer_route(x2d, seq, g, mod, w_q, subkeys)
    return _peer_dense(x2d, seq, h, side, mod, u_tab, v_tab)


def _kvq_kernel(x_ref, gkv_ref, shkv_ref, sckv_ref, gq_ref, shq_ref, scq_ref, wkv_ref, wq_ref,
                cos_ref, sinm_ref, sinp_ref, k_ref, v_ref, q_ref):
    x = x_ref[...]
    cos, sinm, sinp = cos_ref[...], sinm_ref[...], sinp_ref[...]
    hk = _rms_modulate(x, gkv_ref[...], shkv_ref[0], sckv_ref[0])
    kv = _dot(hk, wkv_ref[...])
    nk = k_ref.shape[1]
    k_ref[...] = _apply_rope(kv[:, :nk], cos, sinm, sinp).astype(k_ref.dtype)
    v_ref[...] = kv[:, nk:].astype(v_ref.dtype)
    hq = _rms_modulate(x, gq_ref[...], shq_ref[0], scq_ref[0])
    q_ref[...] = _apply_rope(_dot(hq, wq_ref[...]), cos, sinm, sinp).astype(q_ref.dtype)


def _kvq(x2d, seq, kv_norm_g, kv_mod, kv_w, q_norm_g, mod, w_q, rope):
    t, d = x2d.shape
    tb = TB_PROJ
    spb = seq // tb
    nk = SWA_KV_HEADS * SWA_HD
    nq = SWA_HEADS * SWA_HD
    const = lambda i: (0, 0)
    tok = lambda w: pl.BlockSpec((tb, w), lambda i: (i, 0))
    modspec = lambda k: pl.BlockSpec((1, 1, d), lambda i: (i // spb, 0, k))
    return pl.pallas_call(
        _kvq_kernel,
        grid=(t // tb,),
        in_specs=[tok(d),
                  pl.BlockSpec((1, d), const), modspec(0), modspec(1),
                  pl.BlockSpec((1, d), const), modspec(0), modspec(1),
                  pl.BlockSpec((d, 2 * nk), const), pl.BlockSpec((d, nq), const),
                  tok(LANES), tok(LANES), tok(LANES)],
        out_specs=[tok(nk), tok(nk), tok(nq)],
        out_shape=[jax.ShapeDtypeStruct((t, nk), MXU_DTYPE), jax.ShapeDtypeStruct((t, nk), MXU_DTYPE),
                   jax.ShapeDtypeStruct((t, nq), MXU_DTYPE)],
        compiler_params=_cparams("parallel"),
        name="kv_q_proj",
    )(x2d, kv_norm_g.reshape(1, d), kv_mod, kv_mod, q_norm_g.reshape(1, d), mod, mod,
      kv_w.astype(MXU_DTYPE), w_q.astype(MXU_DTYPE), *rope)


def _swa_kernel(sink_ref, q_ref, kc_ref, kp_ref, vc_ref, vp_ref, x_ref, wout_ref, gate_ref,
                o_ref, att_ref):
    p = SWA_WINDOW
    group = SWA_HEADS // SWA_KV_HEADS
    first = pl.program_id(1) == 0
    lane = lax.broadcasted_iota(jnp.int32, (p, LANES), 1)
    rows = 2 * p
    qi = lax.broadcasted_iota(jnp.int32, (rows, 2 * p), 0) % p
    mi = lax.broadcasted_iota(jnp.int32, (rows, 2 * p), 1)
    band = (mi > qi) & (mi <= qi + p)
    top_half = lax.broadcasted_iota(jnp.int32, (rows, 1), 0) < p
    zero = jnp.zeros((), MXU_DTYPE)
    for jb in range(q_ref.shape[0] // p):
        rs = slice(jb * p, (jb + 1) * p)
        if jb == 0:
            k_prev, v_prev = kp_ref[...], vp_ref[...]
            mask = band & ((mi >= p) | jnp.logical_not(first))
        else:
            ps = slice((jb - 1) * p, jb * p)
            k_prev, v_prev = kc_ref[ps, :], vc_ref[ps, :]
            mask = band
        k_band = jnp.concatenate([k_prev, kc_ref[rs, :]], axis=0)
        v_band = jnp.concatenate([v_prev, vc_ref[rs, :]], axis=0)
        for kh in range(SWA_KV_HEADS):
            kv_tile = k_band[:, (kh // 2) * LANES:(kh // 2 + 1) * LANES]
            vv_tile = v_band[:, (kh // 2) * LANES:(kh // 2 + 1) * LANES]
            mine = (lane[0:1] // SWA_HD) == (kh % 2)
            k_own = jnp.where(mine, kv_tile, zero)
            v_own = jnp.where(mine, vv_tile, zero)
            k_swp = pltpu.roll(k_own.astype(F32), SWA_HD, 1).astype(MXU_DTYPE)
            v_swp = pltpu.roll(v_own.astype(F32), SWA_HD, 1).astype(MXU_DTYPE)
            k_lo, k_hi = (k_own, k_swp) if kh % 2 == 0 else (k_swp, k_own)
            v_lo, v_hi = (v_own, v_swp) if kh % 2 == 0 else (v_swp, v_own)
            t0 = kh * group // 2
            q2 = jnp.concatenate([q_ref[rs, t0 * LANES:(t0 + 1) * LANES],
                                  q_ref[rs, (t0 + 1) * LANES:(t0 + 2) * LANES]], axis=0)
            out = jnp.zeros((rows, LANES), F32)
            for pos, (k_x, v_x) in enumerate(((k_lo, v_lo), (k_hi, v_hi))):
                s = _dot_nt(q2, k_x) * (SWA_HD ** -0.5)
                s = jnp.where(mask, s, NEG_INF)
                sink = jnp.where(top_half, sink_ref[kh * group + pos], sink_ref[kh * group + 2 + pos])
                m = jnp.maximum(jnp.max(s, axis=-1, keepdims=True), sink)
                e = jnp.exp(s - m)
                probs = e / (jnp.sum(e, axis=-1, keepdims=True) + jnp.exp(sink - m))
                out = out + _dot(probs, v_x)
            att_ref[rs, t0 * LANES:(t0 + 1) * LANES] = out[:p].astype(att_ref.dtype)
            att_ref[rs, (t0 + 1) * LANES:(t0 + 2) * LANES] = out[p:].astype(att_ref.dtype)
    y = jnp.dot(att_ref[...], wout_ref[...], preferred_element_type=F32)
    o_ref[...] = x_ref[...] + gate_ref[0] * y


def _swa(x2d, bsz, seq, q, k, v, sinks, w_out, mod):
    t, d = x2d.shape
    tb, p = TB_SWA, SWA_WINDOW
    spb = seq // tb
    nk, nq = k.shape[1], q.shape[1]
    tok = lambda b, s: (b * spb + s, 0)
    prev = lambda b, s: (jnp.maximum((b * spb + s) * (tb // p) - 1, 0), 0)
    return pl.pallas_call(
        _swa_kernel,
        grid=(bsz, spb),
        in_specs=[pl.BlockSpec(memory_space=pltpu.SMEM),
                  pl.BlockSpec((tb, nq), tok),
                  pl.BlockSpec((tb, nk), tok), pl.BlockSpec((p, nk), prev),
                  pl.BlockSpec((tb, nk), tok), pl.BlockSpec((p, nk), prev),
                  pl.BlockSpec((tb, d), tok),
                  pl.BlockSpec((nq, d), lambda b, s: (0, 0)),
                  pl.BlockSpec((1, 1, d), lambda b, s: (b, 0, 2))],
        out_specs=pl.BlockSpec((tb, d), tok),
        out_shape=jax.ShapeDtypeStruct((t, d), F32),
        scratch_shapes=[pltpu.VMEM((tb, nq), MXU_DTYPE)],
        compiler_params=_cparams("parallel", "parallel"),
        name="swa_attention",
    )(sinks, q, k, k, v, v, x2d, w_out.astype(MXU_DTYPE), mod)


def _final_norm_kernel(x_ref, g_ref, o_ref):
    x = x_ref[...]
    o_ref[...] = x * lax.rsqrt(jnp.mean(x * x, axis=-1, keepdims=True) + EPS) * g_ref[...]


def _final_norm(x2d, g):
    t, d = x2d.shape
    tb = TB_PROJ
    return pl.pallas_call(
        _final_norm_kernel,
        grid=(t // tb,),
        in_specs=[pl.BlockSpec((tb, d), lambda i: (i, 0)), pl.BlockSpec((1, d), lambda i: (0, 0))],
        out_specs=pl.BlockSpec((tb, d), lambda i: (i, 0)),
        out_shape=jax.ShapeDtypeStruct((t, d), F32),
        compiler_params=_cparams("parallel"),
        name="final_norm",
    )(x2d, g.reshape(1, d))


def kernel(x, c, positions, mod_w, mod_b, norm_g, gla_w_in, gla_w_g2, gla_b_g2, gla_norm_g, gla_w_out,
           kv_mod_w, kv_mod_b, kv_norm_g, kv_w, swa_w_q, swa_sinks, swa_w_out, peer_w_q, peer_subkeys,
           peer_u, peer_v, final_norm_g):
    bsz, seq, d = x.shape
    x2d = x.reshape(bsz * seq, d)
    mod0 = _mod_vectors(c, mod_w[0], mod_b[0]).reshape(bsz, 1, 6 * d)
    mod1 = _mod_vectors(c, mod_w[1], mod_b[1]).reshape(bsz, 1, 6 * d)
    kv_mod = _mod_vectors(c, kv_mod_w, kv_mod_b).reshape(bsz, 1, 2 * d)
    rope = _rope_tables(positions)

    qkvo, la = _gla_in(x2d, seq, norm_g[0, 0], mod0, gla_w_in[0], gla_w_g2[0], gla_b_g2[0])
    x2d = _gla_core(x2d, bsz, seq, qkvo, la, gla_w_out[0], mod0, gla_norm_g[0])
    x2d = _peer(x2d, seq, norm_g[0, 1], mod0, peer_w_q[0], peer_subkeys[0], peer_u[0], peer_v[0])

    k, v, q = _kvq(x2d, seq, kv_norm_g, kv_mod, kv_w, norm_g[1, 0], mod1, swa_w_q[0], rope)
    x2d = _swa(x2d, bsz, seq, q, k, v, swa_sinks[0], swa_w_out[0], mod1)
    x2d = _peer(x2d, seq, norm_g[1, 1], mod1, peer_w_q[1], peer_subkeys[1], peer_u[1], peer_v[1])

    return _final_norm(x2d, final_norm_g).reshape(bsz, seq, d)
```

```python
import functools

import jax
import jax.numpy as jnp
from jax import lax
from jax.experimental import pallas as pl
from jax.experimental.pallas import tpu as pltpu

F32 = jnp.float32
MXU_DTYPE = jnp.bfloat16
EPS = 1e-6
NEG_INF = float("-inf")
POS_INF = float("inf")

LANES = 128
VMEM_LIMIT_BYTES = 56 * 1024 * 1024

GLA_HEADS, GLA_DK, GLA_DV, GLA_RANK, GLA_TAU, GLA_CHUNK = 4, 128, 256, 16, 16.0, 64
SWA_HEADS, SWA_KV_HEADS, SWA_HD, SWA_WINDOW = 16, 4, 64, 128
ROT_DIM, ROPE_THETA = 16, 500000.0
PEER_HEADS, PEER_KEYS, PEER_TOPK, PEER_QDIM = 8, 128, 16, 256

TB_PROJ = 512
TB_GLA = 256
TB_SWA = 512
TB_PEER = 512
TB_ROUTE = 256
EC_PEER = 1024

_NT = (((1,), (1,)), ((), ()))
_TN = (((0,), (0,)), ((), ()))


def _cparams(*sem):
    return pltpu.CompilerParams(dimension_semantics=sem, vmem_limit_bytes=VMEM_LIMIT_BYTES)


def _rms_modulate(x, g, shift, scale):
    y = x * lax.rsqrt(jnp.mean(x * x, axis=-1, keepdims=True) + EPS) * g
    return y * (1.0 + scale) + shift


def _dot(a, b):
    return jnp.dot(a.astype(MXU_DTYPE), b.astype(MXU_DTYPE), preferred_element_type=F32)


def _dot_nt(a, b):
    return lax.dot_general(a.astype(MXU_DTYPE), b.astype(MXU_DTYPE), _NT, preferred_element_type=F32)


def _mod_kernel(c_ref, w_ref, b_ref, o_ref):
    c = c_ref[...]
    ca = c * jax.nn.sigmoid(c)
    o_ref[...] = jnp.dot(ca, w_ref[...], preferred_element_type=F32,
                         precision=lax.Precision.HIGHEST) + b_ref[...]


def _mod_vectors(c, w, b):
    bsz, d = c.shape
    n = w.shape[1]
    bn = 1024
    return pl.pallas_call(
        _mod_kernel,
        grid=(n // bn,),
        in_specs=[pl.BlockSpec((bsz, d), lambda j: (0, 0)),
                  pl.BlockSpec((d, bn), lambda j: (0, j)),
                  pl.BlockSpec((1, bn), lambda j: (0, j))],
        out_specs=pl.BlockSpec((bsz, bn), lambda j: (0, j)),
        out_shape=jax.ShapeDtypeStruct((bsz, n), F32),
        compiler_params=_cparams("parallel"),
        name="mod_vectors",
    )(c, w, b.reshape(1, n))


def _rope_kernel(pos_ref, inv_ref, cos_ref, sinm_ref, sinp_ref):
    ang = pos_ref[...].astype(F32) * inv_ref[...]
    lane = lax.broadcasted_iota(jnp.int32, ang.shape, 1) % SWA_HD
    lo = lane < ROT_DIM // 2
    hi = (lane >= ROT_DIM // 2) & (lane < ROT_DIM)
    cs, sn = jnp.cos(ang), jnp.sin(ang)
    cos_ref[...] = jnp.where(lo | hi, cs, 1.0)
    sinm_ref[...] = jnp.where(lo, -sn, 0.0)
    sinp_ref[...] = jnp.where(hi, sn, 0.0)


def _rope_tables(positions):
    t = positions.size
    half = ROT_DIM // 2
    inv = ROPE_THETA ** (-jnp.arange(0, ROT_DIM, 2, dtype=F32) / ROT_DIM)
    inv_row = jnp.tile(inv, LANES // half).reshape(1, LANES)
    tb = TB_PROJ
    spec = pl.BlockSpec((tb, LANES), lambda i: (i, 0))
    return pl.pallas_call(
        _rope_kernel,
        grid=(t // tb,),
        in_specs=[pl.BlockSpec((tb, 1), lambda i: (i, 0)), pl.BlockSpec((1, LANES), lambda i: (0, 0))],
        out_specs=[spec, spec, spec],
        out_shape=[jax.ShapeDtypeStruct((t, LANES), F32)] * 3,
        compiler_params=_cparams("parallel"),
        name="rope_tables",
    )(positions.reshape(t, 1), inv_row)


def _apply_rope(x, cos, sinm, sinp):
    outs = []
    for g in range(x.shape[1] // LANES):
        xs = x[:, g * LANES:(g + 1) * LANES]
        outs.append(xs * cos + pltpu.roll(xs, LANES - ROT_DIM // 2, 1) * sinm
                    + pltpu.roll(xs, ROT_DIM // 2, 1) * sinp)
    return jnp.concatenate(outs, axis=1)


def _gla_in_kernel(x_ref, g_ref, sh_ref, sc_ref, wm_ref, wgl_ref, wg2_ref, bg2_ref, qkvo_ref, la_ref):
    h = _rms_modulate(x_ref[...], g_ref[...], sh_ref[0], sc_ref[0]).astype(MXU_DTYPE)
    qkvo_ref[...] = jnp.dot(h, wm_ref[...], preferred_element_type=F32).astype(qkvo_ref.dtype)
    gl = jnp.dot(h, wgl_ref[...], preferred_element_type=F32)
    z = jnp.dot(gl.astype(MXU_DTYPE), wg2_ref[...], preferred_element_type=F32) + bg2_ref[...]
    la_ref[...] = (jnp.minimum(z, 0.0) - jnp.log1p(jnp.exp(-jnp.abs(z)))) * (1.0 / GLA_TAU)


def _gla_in(x2d, seq, g, mod, w_in, w_g2, b_g2):
    t, d = x2d.shape
    tb = TB_PROJ
    spb = seq // tb
    n_main = GLA_HEADS * (2 * GLA_DK + 2 * GLA_DV)
    n_gate = GLA_HEADS * GLA_DK
    w_main = w_in[:, :n_main].astype(MXU_DTYPE)
    w_gl = jnp.pad(w_in[:, n_main:], ((0, 0), (0, LANES - GLA_RANK))).astype(MXU_DTYPE)
    w_g2p = jnp.pad(w_g2, ((0, LANES - GLA_RANK), (0, 0))).astype(MXU_DTYPE)
    const = lambda i: (0, 0)
    return pl.pallas_call(
        _gla_in_kernel,
        grid=(t // tb,),
        in_specs=[pl.BlockSpec((tb, d), lambda i: (i, 0)),
                  pl.BlockSpec((1, d), const),
                  pl.BlockSpec((1, 1, d), lambda i: (i // spb, 0, 0)),
                  pl.BlockSpec((1, 1, d), lambda i: (i // spb, 0, 1)),
                  pl.BlockSpec((d, n_main), const),
                  pl.BlockSpec((d, LANES), const),
                  pl.BlockSpec((LANES, n_gate), const),
                  pl.BlockSpec((1, n_gate), const)],
        out_specs=[pl.BlockSpec((tb, n_main), lambda i: (i, 0)),
                   pl.BlockSpec((tb, n_gate), lambda i: (i, 0))],
        out_shape=[jax.ShapeDtypeStruct((t, n_main), MXU_DTYPE),
                   jax.ShapeDtypeStruct((t, n_gate), F32)],
        compiler_params=_cparams("parallel"),
        name="gla_in_proj",
    )(x2d, g.reshape(1, d), mod, mod, w_main, w_gl, w_g2p, b_g2.reshape(1, n_gate))


def _gla_core_kernel(qkvo_ref, la_ref, x_ref, wout_ref, gate_ref, ng_ref, o_ref, st_ref, gated_ref):
    @pl.when(pl.program_id(1) == 0)
    def _():
        st_ref[...] = jnp.zeros_like(st_ref)

    c_len, dk, dv = GLA_CHUNK, GLA_DK, GLA_DV
    k0, v0, g0 = GLA_HEADS * dk, 2 * GLA_HEADS * dk, 2 * GLA_HEADS * dk + GLA_HEADS * dv
    row = lax.broadcasted_iota(jnp.int32, (c_len, c_len), 0)
    col = lax.broadcasted_iota(jnp.int32, (c_len, c_len), 1)
    tri = row >= col
    tri_f = tri.astype(F32)
    for c in range(x_ref.shape[0] // c_len):
        rs = slice(c * c_len, (c + 1) * c_len)
        for h in range(GLA_HEADS):
            la = la_ref[rs, h * dk:(h + 1) * dk]
            b = jnp.dot(tri_f, la, preferred_element_type=F32, precision=lax.Precision.HIGHEST)
            b_last = b[c_len - 1:c_len, :]
            q = qkvo_ref[rs, h * dk:(h + 1) * dk].astype(F32) * (dk ** -0.5)
            k = qkvo_ref[rs, k0 + h * dk:k0 + (h + 1) * dk].astype(F32)
            v = qkvo_ref[rs, v0 + h * dv:v0 + (h + 1) * dv]
            og = qkvo_ref[rs, g0 + h * dv:g0 + (h + 1) * dv].astype(F32)
            q_t = q * jnp.exp(b)
            k_t = k * jnp.exp(-b)
            k_dec = k * jnp.exp(b_last - b)
            dec = jnp.exp(b_last)
            scores = jnp.where(tri, _dot_nt(q_t, k_t), 0.0)
            st = st_ref[h]
            o = _dot(scores, v) + _dot_nt(q_t, st)
            st_ref[h] = dec * st + lax.dot_general(v.astype(MXU_DTYPE), k_dec.astype(MXU_DTYPE), _TN,
                                                   preferred_element_type=F32)
            o = o * lax.rsqrt(jnp.mean(o * o, axis=-1, keepdims=True) + EPS) * ng_ref[...]
            o = o * (og * jax.nn.sigmoid(og))
            gated_ref[rs, h * dv:(h + 1) * dv] = o.astype(gated_ref.dtype)
    y = jnp.dot(gated_ref[...], wout_ref[...], preferred_element_type=F32)
    o_ref[...] = x_ref[...] + gate_ref[0] * y


def _gla_core(x2d, bsz, seq, qkvo, la, w_out, mod, norm_g):
    t, d = x2d.shape
    tb = TB_GLA
    spb = seq // tb
    n_main, n_gate = qkvo.shape[1], la.shape[1]
    hv = GLA_HEADS * GLA_DV
    tok = lambda b, s: (b * spb + s, 0)
    return pl.pallas_call(
        _gla_core_kernel,
        grid=(bsz, spb),
        in_specs=[pl.BlockSpec((tb, n_main), tok),
                  pl.BlockSpec((tb, n_gate), tok),
                  pl.BlockSpec((tb, d), tok),
                  pl.BlockSpec((hv, d), lambda b, s: (0, 0)),
                  pl.BlockSpec((1, 1, d), lambda b, s: (b, 0, 2)),
                  pl.BlockSpec((1, GLA_DV), lambda b, s: (0, 0))],
        out_specs=pl.BlockSpec((tb, d), tok),
        out_shape=jax.ShapeDtypeStruct((t, d), F32),
        scratch_shapes=[pltpu.VMEM((GLA_HEADS, GLA_DV, GLA_DK), F32),
                        pltpu.VMEM((tb, hv), MXU_DTYPE)],
        compiler_params=_cparams("parallel", "arbitrary"),
        name="gla_core",
    )(qkvo, la, x2d, w_out.astype(MXU_DTYPE), mod, norm_g.reshape(1, GLA_DV))


def _sort16_pairs():
    def merge(lo, hi, r):
        step = r * 2
        if step < hi - lo:
            yield from merge(lo, hi, step)
            yield from merge(lo + r, hi, step)
            yield from [(i, i + r) for i in range(lo + r, hi - r, step)]
        else:
            yield (lo, lo + r)

    def sort(lo, hi):
        if hi - lo >= 1:
            mid = lo + (hi - lo) // 2
            yield from sort(lo, mid)
            yield from sort(mid + 1, hi)
            yield from merge(lo, hi, 1)

    return tuple(sort(0, PEER_TOPK - 1))


_SORT16 = _sort16_pairs()


def _sort16_desc(w):
    w = list(w)
    for i, j in _SORT16:
        w[i], w[j] = jnp.maximum(w[i], w[j]), jnp.minimum(w[i], w[j])
    return w


def _merge_top16(a, b):
    n = PEER_TOPK
    w = [jnp.maximum(a[i], b[n - 1 - i]) for i in range(n)]
    d = n // 2
    while d >= 1:
        for i in range(n):
            if i & d == 0:
                w[i], w[i + d] = jnp.maximum(w[i], w[i + d]), jnp.minimum(w[i], w[i + d])
        d //= 2
    return w


def _top16_desc(wires):
    n = PEER_TOPK
    runs = [_sort16_desc(wires[i:i + n]) for i in range(0, len(wires), n)]
    while len(runs) > 1:
        runs = [_merge_top16(runs[i], runs[i + 1]) for i in range(0, len(runs), 2)]
    return runs[0]


def _route_tile(s1, s2):
    n = PEER_TOPK
    v1, v2 = _top16_desc(s1), _top16_desc(s2)
    run = [v1[0] + v2[b] for b in range(n)]
    for a in range(1, n):
        run = _merge_top16(run, [v1[a] + v2[b] for b in range(n)])
    tau = run[n - 1]
    e1 = [jnp.exp(v - v1[0]) for v in v1]
    e2 = [jnp.exp(v - v2[0]) for v in v2]
    z = jnp.zeros_like(tau)
    thr_rank = []
    for a in range(n):
        za = jnp.zeros_like(tau)
        ta = jnp.full_like(tau, POS_INF)
        for b in range(n):
            sel = (v1[a] + v2[b]) >= tau
            za = za + jnp.where(sel, e2[b], 0.0)
            ta = jnp.where(sel, v2[b], ta)
        z = z + e1[a] * za
        thr_rank.append(ta)
    rz = 1.0 / z
    thr, coef = [], []
    for s in s1:
        t = jnp.full_like(tau, POS_INF)
        for a in range(n - 1, -1, -1):
            t = jnp.where(s == v1[a], thr_rank[a], t)
        thr.append(t)
        coef.append(jnp.exp(s - v1[0]) * rz)
    return thr, coef, v2[0]


def _peer_route_kernel(x_ref, g_ref, sh_ref, sc_ref, wqt_ref, k1i_ref, k2i_ref, k2n_ref,
                       h_ref, thr_ref, coef_ref, s2_ref, e2_ref):
    nk, nh = PEER_KEYS, PEER_HEADS
    h = _rms_modulate(x_ref[...], g_ref[...], sh_ref[0], sc_ref[0]).astype(MXU_DTYPE)
    h_ref[...] = h
    q_t = lax.dot_general(wqt_ref[...], h, _NT, preferred_element_type=F32).astype(MXU_DTYPE)
    half = q_t.shape[0] // 2
    s1i = jnp.dot(k1i_ref[...], q_t[:half], preferred_element_type=F32)
    s2i = jnp.dot(k2i_ref[...], q_t[half:], preferred_element_type=F32)
    s2n = jnp.dot(k2n_ref[...], q_t[half:], preferred_element_type=F32)
    for lt in range(thr_ref.shape[0]):
        ls = slice(lt * LANES, (lt + 1) * LANES)
        w1 = [s1i[k * nh:(k + 1) * nh, ls] for k in range(nk)]
        w2 = [s2i[k * nh:(k + 1) * nh, ls] for k in range(nk)]
        thr, coef, m2 = _route_tile(w1, w2)
        for k in range(nk):
            thr_ref[lt, k] = thr[k]
            coef_ref[lt, k] = coef[k]
        for hh in range(nh):
            s2h = s2n[hh * nk:(hh + 1) * nk, ls]
            s2_ref[lt, hh] = s2h
            e2_ref[lt, hh] = jnp.exp(s2h - m2[hh:hh + 1])


def _peer_route(x2d, seq, g, mod, w_q, subkeys):
    t, d = x2d.shape
    tb = TB_ROUTE
    spb = seq // tb
    nh, nk, half = PEER_HEADS, PEER_KEYS, PEER_QDIM // 2
    nq = nh * half
    wqt = w_q.reshape(d, nh, 2, half).transpose(2, 1, 3, 0).reshape(2 * nq, d).astype(MXU_DTYPE)
    eye = jnp.eye(nh, dtype=subkeys.dtype)
    interleaved = lambda k: (k[:, None, None, :] * eye[None, :, :, None]).reshape(nk * nh, nq).astype(MXU_DTYPE)
    natural = lambda k: (eye[:, None, :, None] * k[None, :, None, :]).reshape(nh * nk, nq).astype(MXU_DTYPE)
    const2 = lambda i: (0, 0)
    tiles = tb // LANES
    side_k = pl.BlockSpec((tiles, nk, nh, LANES), lambda i: (i, 0, 0, 0))
    side_h = pl.BlockSpec((tiles, nh, nk, LANES), lambda i: (i, 0, 0, 0))
    shape_k = jax.ShapeDtypeStruct((t // LANES, nk, nh, LANES), F32)
    shape_h = jax.ShapeDtypeStruct((t // LANES, nh, nk, LANES), F32)
    return pl.pallas_call(
        _peer_route_kernel,
        grid=(t // tb,),
        in_specs=[pl.BlockSpec((tb, d), lambda i: (i, 0)),
                  pl.BlockSpec((1, d), const2),
                  pl.BlockSpec((1, 1, d), lambda i: (i // spb, 0, 3)),
                  pl.BlockSpec((1, 1, d), lambda i: (i // spb, 0, 4)),
                  pl.BlockSpec((2 * nq, d), const2),
                  pl.BlockSpec((nk * nh, nq), const2),
                  pl.BlockSpec((nk * nh, nq), const2),
                  pl.BlockSpec((nh * nk, nq), const2)],
        out_specs=[pl.BlockSpec((tb, d), lambda i: (i, 0)), side_k, side_k, side_h, side_h],
        out_shape=[jax.ShapeDtypeStruct((t, d), MXU_DTYPE), shape_k, shape_k, shape_h, shape_h],
        compiler_params=_cparams("parallel"),
        name="peer_route",
    )(x2d, g.reshape(1, d), mod, mod, wqt, interleaved(subkeys[0]), interleaved(subkeys[1]),
      natural(subkeys[1]))


def _gelu_exact(x):
    return 0.5 * x * (1.0 + lax.erf(x * 0.7071067811865476))


def _peer_dense_kernel(h_ref, u_ref, vt_ref, thr_ref, coef_ref, s2_ref, e2_ref, x_ref, gate_ref,
                       o_ref, st_ref, w_ref, acc_ref):
    c = pl.program_id(1)
    ec, tb = st_ref.shape
    groups = ec // PEER_KEYS

    @pl.when(c == 0)
    def _():
        acc_ref[...] = jnp.zeros_like(acc_ref)

    st_ref[...] = lax.dot_general(u_ref[...], h_ref[...], _NT, preferred_element_type=F32)

    def group_body(j, carry):
        i1 = c * groups + j
        r0 = pl.multiple_of(j * PEER_KEYS, PEER_KEYS)
        for lt in range(tb // LANES):
            ls = slice(lt * LANES, (lt + 1) * LANES)
            gw = jnp.zeros((PEER_KEYS, LANES), F32)
            for hh in range(PEER_HEADS):
                thr = thr_ref[lt, i1, pl.ds(hh, 8, stride=0), :]
                cf = coef_ref[lt, i1, pl.ds(hh, 8, stride=0), :]
                thr = jnp.concatenate([thr] * (PEER_KEYS // 8), axis=0)
                cf = jnp.concatenate([cf] * (PEER_KEYS // 8), axis=0)
                gw = gw + jnp.where(s2_ref[lt, hh] >= thr, e2_ref[lt, hh] * cf, 0.0)
            a = _gelu_exact(st_ref[pl.ds(r0, PEER_KEYS), ls])
            w_ref[pl.ds(r0, PEER_KEYS), ls] = (a * gw).astype(w_ref.dtype)
        return carry

    lax.fori_loop(0, groups, group_body, 0)
    acc_ref[...] += jnp.dot(vt_ref[...], w_ref[...], preferred_element_type=F32)

    @pl.when(c == pl.num_programs(1) - 1)
    def _():
        o_ref[...] = x_ref[...] + gate_ref[0] * acc_ref[...].T


def _peer_dense(x2d, seq, h, side, mod, u_tab, v_tab):
    t, d = x2d.shape
    tb, ec = TB_PEER, EC_PEER
    spb = seq // tb
    n_exp = u_tab.shape[0]
    u = u_tab.astype(MXU_DTYPE)
    vt = v_tab.T.astype(MXU_DTYPE)
    thr, coef, s2, e2 = side
    tiles = tb // LANES
    side_k = pl.BlockSpec((tiles, PEER_KEYS, PEER_HEADS, LANES), lambda i, c: (i, 0, 0, 0))
    side_h = pl.BlockSpec((tiles, PEER_HEADS, PEER_KEYS, LANES), lambda i, c: (i, 0, 0, 0))
    return pl.pallas_call(
        _peer_dense_kernel,
        grid=(t // tb, n_exp // ec),
        in_specs=[pl.BlockSpec((tb, d), lambda i, c: (i, 0)),
                  pl.BlockSpec((ec, d), lambda i, c: (c, 0)),
                  pl.BlockSpec((d, ec), lambda i, c: (0, c)),
                  side_k, side_k, side_h, side_h,
                  pl.BlockSpec((tb, d), lambda i, c: (i, 0)),
                  pl.BlockSpec((1, 1, d), lambda i, c: (i // spb, 0, 5))],
        out_specs=pl.BlockSpec((tb, d), lambda i, c: (i, 0)),
        out_shape=jax.ShapeDtypeStruct((t, d), F32),
        scratch_shapes=[pltpu.VMEM((ec, tb), F32),
                        pltpu.VMEM((ec, tb), MXU_DTYPE),
                        pltpu.VMEM((d, tb), F32)],
        compiler_params=_cparams("parallel", "arbitrary"),
        name="peer_dense",
    )(h, u, vt, thr, coef, s2, e2, x2d, mod)


def _peer(x2d, seq, g, mod, w_q, subkeys, u_tab, v_tab):
    h, *side = _peer_route(x2d, seq, g, mod, w_q, subkeys)
    return _peer_dense(x2d, seq, h, side, mod, u_tab, v_tab)


def _kvq_kernel(x_ref, gkv_ref, shkv_ref, sckv_ref, gq_ref, shq_ref, scq_ref, wkv_ref, wq_ref,
                cos_ref, sinm_ref, sinp_ref, k_ref, v_ref, q_ref):
    x = x_ref[...]
    cos, sinm, sinp = cos_ref[...], sinm_ref[...], sinp_ref[...]
    hk = _rms_modulate(x, gkv_ref[...], shkv_ref[0], sckv_ref[0])
    kv = _dot(hk, wkv_ref[...])
    nk = k_ref.shape[1]
    k_ref[...] = _apply_rope(kv[:, :nk], cos, sinm, sinp).astype(k_ref.dtype)
    v_ref[...] = kv[:, nk:].astype(v_ref.dtype)
    hq = _rms_modulate(x, gq_ref[...], shq_ref[0], scq_ref[0])
    q_ref[...] = _apply_rope(_dot(hq, wq_ref[...]), cos, sinm, sinp).astype(q_ref.dtype)


def _kvq(x2d, seq, kv_norm_g, kv_mod, kv_w, q_norm_g, mod, w_q, rope):
    t, d = x2d.shape
    tb = TB_PROJ
    spb = seq // tb
    nk = SWA_KV_HEADS * SWA_HD
    nq = SWA_HEADS * SWA_HD
    const = lambda i: (0, 0)
    tok = lambda w: pl.BlockSpec((tb, w), lambda i: (i, 0))
    modspec = lambda k: pl.BlockSpec((1, 1, d), lambda i: (i // spb, 0, k))
    return pl.pallas_call(
        _kvq_kernel,
        grid=(t // tb,),
        in_specs=[tok(d),
                  pl.BlockSpec((1, d), const), modspec(0), modspec(1),
                  pl.BlockSpec((1, d), const), modspec(0), modspec(1),
                  pl.BlockSpec((d, 2 * nk), const), pl.BlockSpec((d, nq), const),
                  tok(LANES), tok(LANES), tok(LANES)],
        out_specs=[tok(nk), tok(nk), tok(nq)],
        out_shape=[jax.ShapeDtypeStruct((t, nk), MXU_DTYPE), jax.ShapeDtypeStruct((t, nk), MXU_DTYPE),
                   jax.ShapeDtypeStruct((t, nq), MXU_DTYPE)],
        compiler_params=_cparams("parallel"),
        name="kv_q_proj",
    )(x2d, kv_norm_g.reshape(1, d), kv_mod, kv_mod, q_norm_g.reshape(1, d), mod, mod,
      kv_w.astype(MXU_DTYPE), w_q.astype(MXU_DTYPE), *rope)


def _swa_kernel(sink_ref, q_ref, kc_ref, kp_ref, vc_ref, vp_ref, x_ref, wout_ref, gate_ref,
                o_ref, att_ref):
    p = SWA_WINDOW
    group = SWA_HEADS // SWA_KV_HEADS
    first = pl.program_id(1) == 0
    lane = lax.broadcasted_iota(jnp.int32, (p, LANES), 1)
    rows = 2 * p
    qi = lax.broadcasted_iota(jnp.int32, (rows, 2 * p), 0) % p
    mi = lax.broadcasted_iota(jnp.int32, (rows, 2 * p), 1)
    band = (mi > qi) & (mi <= qi + p)
    top_half = lax.broadcasted_iota(jnp.int32, (rows, 1), 0) < p
    zero = jnp.zeros((), MXU_DTYPE)
    for jb in range(q_ref.shape[0] // p):
        rs = slice(jb * p, (jb + 1) * p)
        if jb == 0:
            k_prev, v_prev = kp_ref[...], vp_ref[...]
            mask = band & ((mi >= p) | jnp.logical_not(first))
        else:
            ps = slice((jb - 1) * p, jb * p)
            k_prev, v_prev = kc_ref[ps, :], vc_ref[ps, :]
            mask = band
        k_band = jnp.concatenate([k_prev, kc_ref[rs, :]], axis=0)
        v_band = jnp.concatenate([v_prev, vc_ref[rs, :]], axis=0)
        for kh in range(SWA_KV_HEADS):
            kv_tile = k_band[:, (kh // 2) * LANES:(kh // 2 + 1) * LANES]
            vv_tile = v_band[:, (kh // 2) * LANES:(kh // 2 + 1) * LANES]
            mine = (lane[0:1] // SWA_HD) == (kh % 2)
            k_own = jnp.where(mine, kv_tile, zero)
            v_own = jnp.where(mine, vv_tile, zero)
            k_swp = pltpu.roll(k_own.astype(F32), SWA_HD, 1).astype(MXU_DTYPE)
            v_swp = pltpu.roll(v_own.astype(F32), SWA_HD, 1).astype(MXU_DTYPE)
            k_lo, k_hi = (k_own, k_swp) if kh % 2 == 0 else (k_swp, k_own)
            v_lo, v_hi = (v_own, v_swp) if kh % 2 == 0 else (v_swp, v_own)
            t0 = kh * group // 2
            q2 = jnp.concatenate([q_ref[rs, t0 * LANES:(t0 + 1) * LANES],
                                  q_ref[rs, (t0 + 1) * LANES:(t0 + 2) * LANES]], axis=0)
            out = jnp.zeros((rows, LANES), F32)
            for pos, (k_x, v_x) in enumerate(((k_lo, v_lo), (k_hi, v_hi))):
                s = _dot_nt(q2, k_x) * (SWA_HD ** -0.5)
                s = jnp.where(mask, s, NEG_INF)
                sink = jnp.where(top_half, sink_ref[kh * group + pos], sink_ref[kh * group + 2 + pos])
                m = jnp.maximum(jnp.max(s, axis=-1, keepdims=True), sink)
                e = jnp.exp(s - m)
                probs = e / (jnp.sum(e, axis=-1, keepdims=True) + jnp.exp(sink - m))
                out = out + _dot(probs, v_x)
            att_ref[rs, t0 * LANES:(t0 + 1) * LANES] = out[:p].astype(att_ref.dtype)
            att_ref[rs, (t0 + 1) * LANES:(t0 + 2) * LANES] = out[p:].astype(att_ref.dtype)
    y = jnp.dot(att_ref[...], wout_ref[...], preferred_element_type=F32)
    o_ref[...] = x_ref[...] + gate_ref[0] * y


def _swa(x2d, bsz, seq, q, k, v, sinks, w_out, mod):
    t, d = x2d.shape
    tb, p = TB_SWA, SWA_WINDOW
    spb = seq // tb
    nk, nq = k.shape[1], q.shape[1]
    tok = lambda b, s: (b * spb + s, 0)
    prev = lambda b, s: (jnp.maximum((b * spb + s) * (tb // p) - 1, 0), 0)
    return pl.pallas_call(
        _swa_kernel,
        grid=(bsz, spb),
        in_specs=[pl.BlockSpec(memory_space=pltpu.SMEM),
                  pl.BlockSpec((tb, nq), tok),
                  pl.BlockSpec((tb, nk), tok), pl.BlockSpec((p, nk), prev),
                  pl.BlockSpec((tb, nk), tok), pl.BlockSpec((p, nk), prev),
                  pl.BlockSpec((tb, d), tok),
                  pl.BlockSpec((nq, d), lambda b, s: (0, 0)),
                  pl.BlockSpec((1, 1, d), lambda b, s: (b, 0, 2))],
        out_specs=pl.BlockSpec((tb, d), tok),
        out_shape=jax.ShapeDtypeStruct((t, d), F32),
        scratch_shapes=[pltpu.VMEM((tb, nq), MXU_DTYPE)],
        compiler_params=_cparams("parallel", "parallel"),
        name="swa_attention",
    )(sinks, q, k, k, v, v, x2d, w_out.astype(MXU_DTYPE), mod)


def _final_norm_kernel(x_ref, g_ref, o_ref):
    x = x_ref[...]
    o_ref[...] = x * lax.rsqrt(jnp.mean(x * x, axis=-1, keepdims=True) + EPS) * g_ref[...]


def _final_norm(x2d, g):
    t, d = x2d.shape
    tb = TB_PROJ
    return pl.pallas_call(
        _final_norm_kernel,
        grid=(t // tb,),
        in_specs=[pl.BlockSpec((tb, d), lambda i: (i, 0)), pl.BlockSpec((1, d), lambda i: (0, 0))],
        out_specs=pl.BlockSpec((tb, d), lambda i: (i, 0)),
        out_shape=jax.ShapeDtypeStruct((t, d), F32),
        compiler_params=_cparams("parallel"),
        name="final_norm",
    )(x2d, g.reshape(1, d))


def kernel(x, c, positions, mod_w, mod_b, norm_g, gla_w_in, gla_w_g2, gla_b_g2, gla_norm_g, gla_w_out,
           kv_mod_w, kv_mod_b, kv_norm_g, kv_w, swa_w_q, swa_sinks, swa_w_out, peer_w_q, peer_subkeys,
           peer_u, peer_v, final_norm_g):
    bsz, seq, d = x.shape
    x2d = x.reshape(bsz * seq, d)
    mod0 = _mod_vectors(c, mod_w[0], mod_b[0]).reshape(bsz, 1, 6 * d)
    mod1 = _mod_vectors(c, mod_w[1], mod_b[1]).reshape(bsz, 1, 6 * d)
    kv_mod = _mod_vectors(c, kv_mod_w, kv_mod_b).reshape(bsz, 1, 2 * d)
    rope = _rope_tables(positions)

    qkvo, la = _gla_in(x2d, seq, norm_g[0, 0], mod0, gla_w_in[0], gla_w_g2[0], gla_b_g2[0])
    x2d = _gla_core(x2d, bsz, seq, qkvo, la, gla_w_out[0], mod0, gla_norm_g[0])
    x2d = _peer(x2d, seq, norm_g[0, 1], mod0, peer_w_q[0], peer_subkeys[0], peer_u[0], peer_v[0])

    k, v, q = _kvq(x2d, seq, kv_norm_g, kv_mod, kv_w, norm_g[1, 0], mod1, swa_w_q[0], rope)
    x2d = _swa(x2d, bsz, seq, q, k, v, swa_sinks[0], swa_w_out[0], mod1)
    x2d = _peer(x2d, seq, norm_g[1, 1], mod1, peer_w_q[1], peer_subkeys[1], peer_u[1], peer_v[1])

    return _final_norm(x2d, final_norm_g).reshape(bsz, seq, d)
```

```python
import functools

import jax
import jax.numpy as jnp
from jax import lax
from jax.experimental import pallas as pl
from jax.experimental.pallas import tpu as pltpu

F32 = jnp.float32
MXU_DTYPE = jnp.bfloat16
EPS = 1e-6
NEG_INF = float("-inf")
POS_INF = float("inf")

LANES = 128
VMEM_LIMIT_BYTES = 56 * 1024 * 1024

GLA_HEADS, GLA_DK, GLA_DV, GLA_RANK, GLA_TAU, GLA_CHUNK = 4, 128, 256, 16, 16.0, 64
SWA_HEADS, SWA_KV_HEADS, SWA_HD, SWA_WINDOW = 16, 4, 64, 128
ROT_DIM, ROPE_THETA = 16, 500000.0
PEER_HEADS, PEER_KEYS, PEER_TOPK, PEER_QDIM = 8, 128, 16, 256

TB_PROJ = 512
TB_GLA = 256
TB_SWA = 512
TB_PEER = 512
TB_ROUTE = 256
EC_PEER = 1024

_NT = (((1,), (1,)), ((), ()))
_TN = (((0,), (0,)), ((), ()))


def _cparams(*sem):
    return pltpu.CompilerParams(dimension_semantics=sem, vmem_limit_bytes=VMEM_LIMIT_BYTES)


def _rms_modulate(x, g, shift, scale):
    y = x * lax.rsqrt(jnp.mean(x * x, axis=-1, keepdims=True) + EPS) * g
    return y * (1.0 + scale) + shift


def _dot(a, b):
    return jnp.dot(a.astype(MXU_DTYPE), b.astype(MXU_DTYPE), preferred_element_type=F32)


def _dot_nt(a, b):
    return lax.dot_general(a.astype(MXU_DTYPE), b.astype(MXU_DTYPE), _NT, preferred_element_type=F32)


def _mod_kernel(c_ref, w_ref, b_ref, o_ref):
    c = c_ref[...]
    ca = c * jax.nn.sigmoid(c)
    o_ref[...] = jnp.dot(ca, w_ref[...], preferred_element_type=F32,
                         precision=lax.Precision.HIGHEST) + b_ref[...]


def _mod_vectors(c, w, b):
    bsz, d = c.shape
    n = w.shape[1]
    bn = 1024
    return pl.pallas_call(
        _mod_kernel,
        grid=(n // bn,),
        in_specs=[pl.BlockSpec((bsz, d), lambda j: (0, 0)),
                  pl.BlockSpec((d, bn), lambda j: (0, j)),
                  pl.BlockSpec((1, bn), lambda j: (0, j))],
        out_specs=pl.BlockSpec((bsz, bn), lambda j: (0, j)),
        out_shape=jax.ShapeDtypeStruct((bsz, n), F32),
        compiler_params=_cparams("parallel"),
        name="mod_vectors",
    )(c, w, b.reshape(1, n))


def _rope_kernel(pos_ref, inv_ref, cos_ref, sinm_ref, sinp_ref):
    ang = pos_ref[...].astype(F32) * inv_ref[...]
    lane = lax.broadcasted_iota(jnp.int32, ang.shape, 1) % SWA_HD
    lo = lane < ROT_DIM // 2
    hi = (lane >= ROT_DIM // 2) & (lane < ROT_DIM)
    cs, sn = jnp.cos(ang), jnp.sin(ang)
    cos_ref[...] = jnp.where(lo | hi, cs, 1.0)
    sinm_ref[...] = jnp.where(lo, -sn, 0.0)
    sinp_ref[...] = jnp.where(hi, sn, 0.0)


def _rope_tables(positions):
    t = positions.size
    half = ROT_DIM // 2
    inv = ROPE_THETA ** (-jnp.arange(0, ROT_DIM, 2, dtype=F32) / ROT_DIM)
    inv_row = jnp.tile(inv, LANES // half).reshape(1, LANES)
    tb = TB_PROJ
    spec = pl.BlockSpec((tb, LANES), lambda i: (i, 0))
    return pl.pallas_call(
        _rope_kernel,
        grid=(t // tb,),
        in_specs=[pl.BlockSpec((tb, 1), lambda i: (i, 0)), pl.BlockSpec((1, LANES), lambda i: (0, 0))],
        out_specs=[spec, spec, spec],
        out_shape=[jax.ShapeDtypeStruct((t, LANES), F32)] * 3,
        compiler_params=_cparams("parallel"),
        name="rope_tables",
    )(positions.reshape(t, 1), inv_row)


def _apply_rope(x, cos, sinm, sinp):
    outs = []
    for g in range(x.shape[1] // LANES):
        xs = x[:, g * LANES:(g + 1) * LANES]
        outs.append(xs * cos + pltpu.roll(xs, LANES - ROT_DIM // 2, 1) * sinm
                    + pltpu.roll(xs, ROT_DIM // 2, 1) * sinp)
    return jnp.concatenate(outs, axis=1)


def _gla_in_kernel(x_ref, g_ref, sh_ref, sc_ref, wm_ref, wgl_ref, wg2_ref, bg2_ref, qkvo_ref, la_ref):
    h = _rms_modulate(x_ref[...], g_ref[...], sh_ref[0], sc_ref[0]).astype(MXU_DTYPE)
    qkvo_ref[...] = jnp.dot(h, wm_ref[...], preferred_element_type=F32).astype(qkvo_ref.dtype)
    gl = jnp.dot(h, wgl_ref[...], preferred_element_type=F32)
    z = jnp.dot(gl.astype(MXU_DTYPE), wg2_ref[...], preferred_element_type=F32) + bg2_ref[...]
    la_ref[...] = (jnp.minimum(z, 0.0) - jnp.log1p(jnp.exp(-jnp.abs(z)))) * (1.0 / GLA_TAU)


def _gla_in(x2d, seq, g, mod, w_in, w_g2, b_g2):
    t, d = x2d.shape
    tb = TB_PROJ
    spb = seq // tb
    n_main = GLA_HEADS * (2 * GLA_DK + 2 * GLA_DV)
    n_gate = GLA_HEADS * GLA_DK
    w_main = w_in[:, :n_main].astype(MXU_DTYPE)
    w_gl = jnp.pad(w_in[:, n_main:], ((0, 0), (0, LANES - GLA_RANK))).astype(MXU_DTYPE)
    w_g2p = jnp.pad(w_g2, ((0, LANES - GLA_RANK), (0, 0))).astype(MXU_DTYPE)
    const = lambda i: (0, 0)
    return pl.pallas_call(
        _gla_in_kernel,
        grid=(t // tb,),
        in_specs=[pl.BlockSpec((tb, d), lambda i: (i, 0)),
                  pl.BlockSpec((1, d), const),
                  pl.BlockSpec((1, 1, d), lambda i: (i // spb, 0, 0)),
                  pl.BlockSpec((1, 1, d), lambda i: (i // spb, 0, 1)),
                  pl.BlockSpec((d, n_main), const),
                  pl.BlockSpec((d, LANES), const),
                  pl.BlockSpec((LANES, n_gate), const),
                  pl.BlockSpec((1, n_gate), const)],
        out_specs=[pl.BlockSpec((tb, n_main), lambda i: (i, 0)),
                   pl.BlockSpec((tb, n_gate), lambda i: (i, 0))],
        out_shape=[jax.ShapeDtypeStruct((t, n_main), MXU_DTYPE),
                   jax.ShapeDtypeStruct((t, n_gate), F32)],
        compiler_params=_cparams("parallel"),
        name="gla_in_proj",
    )(x2d, g.reshape(1, d), mod, mod, w_main, w_gl, w_g2p, b_g2.reshape(1, n_gate))


def _gla_core_kernel(qkvo_ref, la_ref, x_ref, wout_ref, gate_ref, ng_ref, o_ref, st_ref, gated_ref):
    @pl.when(pl.program_id(1) == 0)
    def _():
        st_ref[...] = jnp.zeros_like(st_ref)

    c_len, dk, dv = GLA_CHUNK, GLA_DK, GLA_DV
    k0, v0, g0 = GLA_HEADS * dk, 2 * GLA_HEADS * dk, 2 * GLA_HEADS * dk + GLA_HEADS * dv
    tb = x_ref.shape[0]
    n_c = tb // c_len
    row = lax.broadcasted_iota(jnp.int32, (tb, tb), 0)
    col = lax.broadcasted_iota(jnp.int32, (tb, tb), 1)
    tri = (row // c_len == col // c_len) & (row >= col)
    b_all = jnp.dot(tri.astype(F32), la_ref[...], preferred_element_type=F32,
                    precision=lax.Precision.HIGHEST)
    for h in range(GLA_HEADS):
        b = b_all[:, h * dk:(h + 1) * dk]
        lasts = [b[(c + 1) * c_len - 1:(c + 1) * c_len, :] for c in range(n_c)]
        b_last = jnp.concatenate([jnp.broadcast_to(bl, (c_len, dk)) for bl in lasts], axis=0)
        q = qkvo_ref[:, h * dk:(h + 1) * dk].astype(F32) * (dk ** -0.5)
        k = qkvo_ref[:, k0 + h * dk:k0 + (h + 1) * dk].astype(F32)
        v = qkvo_ref[:, v0 + h * dv:v0 + (h + 1) * dv]
        og = qkvo_ref[:, g0 + h * dv:g0 + (h + 1) * dv].astype(F32)
        q_t = (q * jnp.exp(b)).astype(MXU_DTYPE)
        k_t = k * jnp.exp(-b)
        k_dec = (k * jnp.exp(b_last - b)).astype(MXU_DTYPE)
        scores = jnp.where(tri, _dot_nt(q_t, k_t), 0.0)
        o_intra = _dot(scores, v)
        st = st_ref[h]
        o_inter = []
        for c in range(n_c):
            rs = slice(c * c_len, (c + 1) * c_len)
            o_inter.append(_dot_nt(q_t[rs], st))
            st = jnp.exp(lasts[c]) * st + lax.dot_general(v[rs], k_dec[rs], _TN, preferred_element_type=F32)
        st_ref[h] = st
        o = o_intra + jnp.concatenate(o_inter, axis=0)
        o = o * lax.rsqrt(jnp.mean(o * o, axis=-1, keepdims=True) + EPS) * ng_ref[...]
        o = o * (og * jax.nn.sigmoid(og))
        gated_ref[:, h * dv:(h + 1) * dv] = o.astype(gated_ref.dtype)
    y = jnp.dot(gated_ref[...], wout_ref[...], preferred_element_type=F32)
    o_ref[...] = x_ref[...] + gate_ref[0] * y


def _gla_core(x2d, bsz, seq, qkvo, la, w_out, mod, norm_g):
    t, d = x2d.shape
    tb = TB_GLA
    spb = seq // tb
    n_main, n_gate = qkvo.shape[1], la.shape[1]
    hv = GLA_HEADS * GLA_DV
    tok = lambda b, s: (b * spb + s, 0)
    return pl.pallas_call(
        _gla_core_kernel,
        grid=(bsz, spb),
        in_specs=[pl.BlockSpec((tb, n_main), tok),
                  pl.BlockSpec((tb, n_gate), tok),
                  pl.BlockSpec((tb, d), tok),
                  pl.BlockSpec((hv, d), lambda b, s: (0, 0)),
                  pl.BlockSpec((1, 1, d), lambda b, s: (b, 0, 2)),
                  pl.BlockSpec((1, GLA_DV), lambda b, s: (0, 0))],
        out_specs=pl.BlockSpec((tb, d), tok),
        out_shape=jax.ShapeDtypeStruct((t, d), F32),
        scratch_shapes=[pltpu.VMEM((GLA_HEADS, GLA_DV, GLA_DK), F32),
                        pltpu.VMEM((tb, hv), MXU_DTYPE)],
        compiler_params=_cparams("parallel", "arbitrary"),
        name="gla_core",
    )(qkvo, la, x2d, w_out.astype(MXU_DTYPE), mod, norm_g.reshape(1, GLA_DV))


def _sort16_pairs():
    def merge(lo, hi, r):
        step = r * 2
        if step < hi - lo:
            yield from merge(lo, hi, step)
            yield from merge(lo + r, hi, step)
            yield from [(i, i + r) for i in range(lo + r, hi - r, step)]
        else:
            yield (lo, lo + r)

    def sort(lo, hi):
        if hi - lo >= 1:
            mid = lo + (hi - lo) // 2
            yield from sort(lo, mid)
            yield from sort(mid + 1, hi)
            yield from merge(lo, hi, 1)

    return tuple(sort(0, PEER_TOPK - 1))


_SORT16 = _sort16_pairs()


def _sort16_desc(w):
    w = list(w)
    for i, j in _SORT16:
        w[i], w[j] = jnp.maximum(w[i], w[j]), jnp.minimum(w[i], w[j])
    return w


def _merge_top16(a, b):
    n = PEER_TOPK
    w = [jnp.maximum(a[i], b[n - 1 - i]) for i in range(n)]
    d = n // 2
    while d >= 1:
        for i in range(n):
            if i & d == 0:
                w[i], w[i + d] = jnp.maximum(w[i], w[i + d]), jnp.minimum(w[i], w[i + d])
        d //= 2
    return w


def _top16_desc(wires):
    n = PEER_TOPK
    runs = [_sort16_desc(wires[i:i + n]) for i in range(0, len(wires), n)]
    while len(runs) > 1:
        runs = [_merge_top16(runs[i], runs[i + 1]) for i in range(0, len(runs), 2)]
    return runs[0]


def _route_tile(s1, s2):
    n = PEER_TOPK
    v1, v2 = _top16_desc(s1), _top16_desc(s2)
    run = [v1[0] + v2[b] for b in range(n)]
    for a in range(1, n):
        run = _merge_top16(run, [v1[a] + v2[b] for b in range(n)])
    tau = run[n - 1]
    e1 = [jnp.exp(v - v1[0]) for v in v1]
    e2 = [jnp.exp(v - v2[0]) for v in v2]
    z = jnp.zeros_like(tau)
    thr_rank = []
    for a in range(n):
        za = jnp.zeros_like(tau)
        ta = jnp.full_like(tau, POS_INF)
        for b in range(n):
            sel = (v1[a] + v2[b]) >= tau
            za = za + jnp.where(sel, e2[b], 0.0)
            ta = jnp.where(sel, v2[b], ta)
        z = z + e1[a] * za
        thr_rank.append(ta)
    rz = 1.0 / z
    thr, coef = [], []
    for s in s1:
        t = jnp.full_like(tau, POS_INF)
        for a in range(n - 1, -1, -1):
            t = jnp.where(s == v1[a], thr_rank[a], t)
        thr.append(t)
        coef.append(jnp.exp(s - v1[0]) * rz)
    return thr, coef, v2[0]


def _peer_route_kernel(x_ref, g_ref, sh_ref, sc_ref, wqt_ref, k1i_ref, k2i_ref, k2n_ref,
                       h_ref, thr_ref, coef_ref, s2_ref, e2_ref):
    nk, nh = PEER_KEYS, PEER_HEADS
    h = _rms_modulate(x_ref[...], g_ref[...], sh_ref[0], sc_ref[0]).astype(MXU_DTYPE)
    h_ref[...] = h
    q_t = lax.dot_general(wqt_ref[...], h, _NT, preferred_element_type=F32).astype(MXU_DTYPE)
    half = q_t.shape[0] // 2
    s1i = jnp.dot(k1i_ref[...], q_t[:half], preferred_element_type=F32)
    s2i = jnp.dot(k2i_ref[...], q_t[half:], preferred_element_type=F32)
    s2n = jnp.dot(k2n_ref[...], q_t[half:], preferred_element_type=F32)
    for lt in range(thr_ref.shape[0]):
        ls = slice(lt * LANES, (lt + 1) * LANES)
        w1 = [s1i[k * nh:(k + 1) * nh, ls] for k in range(nk)]
        w2 = [s2i[k * nh:(k + 1) * nh, ls] for k in range(nk)]
        thr, coef, m2 = _route_tile(w1, w2)
        for k in range(nk):
            thr_ref[lt, k] = thr[k]
            coef_ref[lt, k] = coef[k]
        for hh in range(nh):
            s2h = s2n[hh * nk:(hh + 1) * nk, ls]
            s2_ref[lt, hh] = s2h
            e2_ref[lt, hh] = jnp.exp(s2h - m2[hh:hh + 1])


def _peer_route(x2d, seq, g, mod, w_q, subkeys):
    t, d = x2d.shape
    tb = TB_ROUTE
    spb = seq // tb
    nh, nk, half = PEER_HEADS, PEER_KEYS, PEER_QDIM // 2
    nq = nh * half
    wqt = w_q.reshape(d, nh, 2, half).transpose(2, 1, 3, 0).reshape(2 * nq, d).astype(MXU_DTYPE)
    eye = jnp.eye(nh, dtype=subkeys.dtype)
    interleaved = lambda k: (k[:, None, None, :] * eye[None, :, :, None]).reshape(nk * nh, nq).astype(MXU_DTYPE)
    natural = lambda k: (eye[:, None, :, None] * k[None, :, None, :]).reshape(nh * nk, nq).astype(MXU_DTYPE)
    const2 = lambda i: (0, 0)
    tiles = tb // LANES
    side_k = pl.BlockSpec((tiles, nk, nh, LANES), lambda i: (i, 0, 0, 0))
    side_h = pl.BlockSpec((tiles, nh, nk, LANES), lambda i: (i, 0, 0, 0))
    shape_k = jax.ShapeDtypeStruct((t // LANES, nk, nh, LANES), F32)
    shape_h = jax.ShapeDtypeStruct((t // LANES, nh, nk, LANES), F32)
    return pl.pallas_call(
        _peer_route_kernel,
        grid=(t // tb,),
        in_specs=[pl.BlockSpec((tb, d), lambda i: (i, 0)),
                  pl.BlockSpec((1, d), const2),
                  pl.BlockSpec((1, 1, d), lambda i: (i // spb, 0, 3)),
                  pl.BlockSpec((1, 1, d), lambda i: (i // spb, 0, 4)),
                  pl.BlockSpec((2 * nq, d), const2),
                  pl.BlockSpec((nk * nh, nq), const2),
                  pl.BlockSpec((nk * nh, nq), const2),
                  pl.BlockSpec((nh * nk, nq), const2)],
        out_specs=[pl.BlockSpec((tb, d), lambda i: (i, 0)), side_k, side_k, side_h, side_h],
        out_shape=[jax.ShapeDtypeStruct((t, d), MXU_DTYPE), shape_k, shape_k, shape_h, shape_h],
        compiler_params=_cparams("parallel"),
        name="peer_route",
    )(x2d, g.reshape(1, d), mod, mod, wqt, interleaved(subkeys[0]), interleaved(subkeys[1]),
      natural(subkeys[1]))


def _gelu_exact(x):
    return 0.5 * x * (1.0 + lax.erf(x * 0.7071067811865476))


def _peer_dense_kernel(h_ref, u_ref, vt_ref, thr_ref, coef_ref, s2_ref, e2_ref, x_ref, gate_ref, fg_ref,
                       o_ref, st_ref, w_ref, acc_ref, *, final_norm):
    c = pl.program_id(1)
    ec, tb = st_ref.shape
    groups = ec // PEER_KEYS

    @pl.when(c == 0)
    def _():
        acc_ref[...] = jnp.zeros_like(acc_ref)

    st_ref[...] = lax.dot_general(u_ref[...], h_ref[...], _NT, preferred_element_type=F32)

    def group_body(j, carry):
        i1 = c * groups + j
        r0 = pl.multiple_of(j * PEER_KEYS, PEER_KEYS)
        for lt in range(tb // LANES):
            ls = slice(lt * LANES, (lt + 1) * LANES)
            gw = jnp.zeros((PEER_KEYS, LANES), F32)
            for hh in range(PEER_HEADS):
                thr = thr_ref[lt, i1, pl.ds(hh, 8, stride=0), :]
                cf = coef_ref[lt, i1, pl.ds(hh, 8, stride=0), :]
                thr = jnp.concatenate([thr] * (PEER_KEYS // 8), axis=0)
                cf = jnp.concatenate([cf] * (PEER_KEYS // 8), axis=0)
                gw = gw + jnp.where(s2_ref[lt, hh] >= thr, e2_ref[lt, hh] * cf, 0.0)
            a = _gelu_exact(st_ref[pl.ds(r0, PEER_KEYS), ls])
            w_ref[pl.ds(r0, PEER_KEYS), ls] = (a * gw).astype(w_ref.dtype)
        return carry

    lax.fori_loop(0, groups, group_body, 0)
    acc_ref[...] += jnp.dot(vt_ref[...], w_ref[...], preferred_element_type=F32)

    @pl.when(c == pl.num_programs(1) - 1)
    def _():
        y = x_ref[...] + gate_ref[0] * acc_ref[...].T
        if final_norm:
            y = y * lax.rsqrt(jnp.mean(y * y, axis=-1, keepdims=True) + EPS) * fg_ref[...]
        o_ref[...] = y


def _peer_dense(x2d, seq, h, side, mod, u_tab, v_tab, final_g, final_norm):
    t, d = x2d.shape
    tb, ec = TB_PEER, EC_PEER
    spb = seq // tb
    n_exp = u_tab.shape[0]
    u = u_tab.astype(MXU_DTYPE)
    vt = v_tab.T.astype(MXU_DTYPE)
    thr, coef, s2, e2 = side
    tiles = tb // LANES
    side_k = pl.BlockSpec((tiles, PEER_KEYS, PEER_HEADS, LANES), lambda i, c: (i, 0, 0, 0))
    side_h = pl.BlockSpec((tiles, PEER_HEADS, PEER_KEYS, LANES), lambda i, c: (i, 0, 0, 0))
    return pl.pallas_call(
        functools.partial(_peer_dense_kernel, final_norm=final_norm),
        grid=(t // tb, n_exp // ec),
        in_specs=[pl.BlockSpec((tb, d), lambda i, c: (i, 0)),
                  pl.BlockSpec((ec, d), lambda i, c: (c, 0)),
                  pl.BlockSpec((d, ec), lambda i, c: (0, c)),
                  side_k, side_k, side_h, side_h,
                  pl.BlockSpec((tb, d), lambda i, c: (i, 0)),
                  pl.BlockSpec((1, 1, d), lambda i, c: (i // spb, 0, 5)),
                  pl.BlockSpec((1, d), lambda i, c: (0, 0))],
        out_specs=pl.BlockSpec((tb, d), lambda i, c: (i, 0)),
        out_shape=jax.ShapeDtypeStruct((t, d), F32),
        scratch_shapes=[pltpu.VMEM((ec, tb), F32),
                        pltpu.VMEM((ec, tb), MXU_DTYPE),
                        pltpu.VMEM((d, tb), F32)],
        compiler_params=_cparams("parallel", "arbitrary"),
        name="peer_dense",
    )(h, u, vt, thr, coef, s2, e2, x2d, mod, final_g.reshape(1, d))


def _peer(x2d, seq, g, mod, w_q, subkeys, u_tab, v_tab, final_g, final_norm):
    h, *side = _peer_route(x2d, seq, g, mod, w_q, subkeys)
    return _peer_dense(x2d, seq, h, side, mod, u_tab, v_tab, final_g, final_norm)


def _kvq_kernel(x_ref, gkv_ref, shkv_ref, sckv_ref, gq_ref, shq_ref, scq_ref, wkv_ref, wq_ref,
                cos_ref, sinm_ref, sinp_ref, k_ref, v_ref, q_ref):
    x = x_ref[...]
    cos, sinm, sinp = cos_ref[...], sinm_ref[...], sinp_ref[...]
    hk = _rms_modulate(x, gkv_ref[...], shkv_ref[0], sckv_ref[0])
    kv = _dot(hk, wkv_ref[...])
    nk = k_ref.shape[1]
    k_ref[...] = _apply_rope(kv[:, :nk], cos, sinm, sinp).astype(k_ref.dtype)
    v_ref[...] = kv[:, nk:].astype(v_ref.dtype)
    hq = _rms_modulate(x, gq_ref[...], shq_ref[0], scq_ref[0])
    q_ref[...] = (_apply_rope(_dot(hq, wq_ref[...]), cos, sinm, sinp) * (SWA_HD ** -0.5)).astype(q_ref.dtype)


def _kvq(x2d, seq, kv_norm_g, kv_mod, kv_w, q_norm_g, mod, w_q, rope):
    t, d = x2d.shape
    tb = TB_PROJ
    spb = seq // tb
    nk = SWA_KV_HEADS * SWA_HD
    nq = SWA_HEADS * SWA_HD
    const = lambda i: (0, 0)
    tok = lambda w: pl.BlockSpec((tb, w), lambda i: (i, 0))
    modspec = lambda k: pl.BlockSpec((1, 1, d), lambda i: (i // spb, 0, k))
    return pl.pallas_call(
        _kvq_kernel,
        grid=(t // tb,),
        in_specs=[tok(d),
                  pl.BlockSpec((1, d), const), modspec(0), modspec(1),
                  pl.BlockSpec((1, d), const), modspec(0), modspec(1),
                  pl.BlockSpec((d, 2 * nk), const), pl.BlockSpec((d, nq), const),
                  tok(LANES), tok(LANES), tok(LANES)],
        out_specs=[tok(nk), tok(nk), tok(nq)],
        out_shape=[jax.ShapeDtypeStruct((t, nk), MXU_DTYPE), jax.ShapeDtypeStruct((t, nk), MXU_DTYPE),
                   jax.ShapeDtypeStruct((t, nq), MXU_DTYPE)],
        compiler_params=_cparams("parallel"),
        name="kv_q_proj",
    )(x2d, kv_norm_g.reshape(1, d), kv_mod, kv_mod, q_norm_g.reshape(1, d), mod, mod,
      kv_w.astype(MXU_DTYPE), w_q.astype(MXU_DTYPE), *rope)


def _swa_kernel(sink_ref, q_ref, kc_ref, kp_ref, vc_ref, vp_ref, x_ref, wout_ref, gate_ref,
                o_ref, att_ref):
    p = SWA_WINDOW
    group = SWA_HEADS // SWA_KV_HEADS
    first = pl.program_id(1) == 0
    lane = lax.broadcasted_iota(jnp.int32, (p, LANES), 1)
    rows = 2 * p
    qi = lax.broadcasted_iota(jnp.int32, (rows, 2 * p), 0) % p
    mi = lax.broadcasted_iota(jnp.int32, (rows, 2 * p), 1)
    band = (mi > qi) & (mi <= qi + p)
    top_half = lax.broadcasted_iota(jnp.int32, (rows, 1), 0) < p
    zero = jnp.zeros((), MXU_DTYPE)
    for jb in range(q_ref.shape[0] // p):
        rs = slice(jb * p, (jb + 1) * p)
        if jb == 0:
            k_prev, v_prev = kp_ref[...], vp_ref[...]
            mask = band & ((mi >= p) | jnp.logical_not(first))
        else:
            ps = slice((jb - 1) * p, jb * p)
            k_prev, v_prev = kc_ref[ps, :], vc_ref[ps, :]
            mask = band
        k_band = jnp.concatenate([k_prev, kc_ref[rs, :]], axis=0)
        v_band = jnp.concatenate([v_prev, vc_ref[rs, :]], axis=0)
        for kh in range(SWA_KV_HEADS):
            kv_tile = k_band[:, (kh // 2) * LANES:(kh // 2 + 1) * LANES]
            vv_tile = v_band[:, (kh // 2) * LANES:(kh // 2 + 1) * LANES]
            mine = (lane[0:1] // SWA_HD) == (kh % 2)
            k_own = jnp.where(mine, kv_tile, zero)
            v_own = jnp.where(mine, vv_tile, zero)
            k_swp = pltpu.roll(k_own.astype(F32), SWA_HD, 1).astype(MXU_DTYPE)
            v_swp = pltpu.roll(v_own.astype(F32), SWA_HD, 1).astype(MXU_DTYPE)
            k_lo, k_hi = (k_own, k_swp) if kh % 2 == 0 else (k_swp, k_own)
            v_lo, v_hi = (v_own, v_swp) if kh % 2 == 0 else (v_swp, v_own)
            t0 = kh * group // 2
            q2 = jnp.concatenate([q_ref[rs, t0 * LANES:(t0 + 1) * LANES],
                                  q_ref[rs, (t0 + 1) * LANES:(t0 + 2) * LANES]], axis=0)
            out = jnp.zeros((rows, LANES), F32)
            for pos, (k_x, v_x) in enumerate(((k_lo, v_lo), (k_hi, v_hi))):
                s = jnp.where(mask, _dot_nt(q2, k_x), NEG_INF)
                sink = jnp.where(top_half, sink_ref[kh * group + pos], sink_ref[kh * group + 2 + pos])
                m = jnp.maximum(jnp.max(s, axis=-1, keepdims=True), sink)
                e = jnp.exp(s - m)
                denom = jnp.sum(e, axis=-1, keepdims=True) + jnp.exp(sink - m)
                out = out + _dot(e, v_x) * (1.0 / denom)
            att_ref[rs, t0 * LANES:(t0 + 1) * LANES] = out[:p].astype(att_ref.dtype)
            att_ref[rs, (t0 + 1) * LANES:(t0 + 2) * LANES] = out[p:].astype(att_ref.dtype)
    y = jnp.dot(att_ref[...], wout_ref[...], preferred_element_type=F32)
    o_ref[...] = x_ref[...] + gate_ref[0] * y


def _swa(x2d, bsz, seq, q, k, v, sinks, w_out, mod):
    t, d = x2d.shape
    tb, p = TB_SWA, SWA_WINDOW
    spb = seq // tb
    nk, nq = k.shape[1], q.shape[1]
    tok = lambda b, s: (b * spb + s, 0)
    prev = lambda b, s: (jnp.maximum((b * spb + s) * (tb // p) - 1, 0), 0)
    return pl.pallas_call(
        _swa_kernel,
        grid=(bsz, spb),
        in_specs=[pl.BlockSpec(memory_space=pltpu.SMEM),
                  pl.BlockSpec((tb, nq), tok),
                  pl.BlockSpec((tb, nk), tok), pl.BlockSpec((p, nk), prev),
                  pl.BlockSpec((tb, nk), tok), pl.BlockSpec((p, nk), prev),
                  pl.BlockSpec((tb, d), tok),
                  pl.BlockSpec((nq, d), lambda b, s: (0, 0)),
                  pl.BlockSpec((1, 1, d), lambda b, s: (b, 0, 2))],
        out_specs=pl.BlockSpec((tb, d), tok),
        out_shape=jax.ShapeDtypeStruct((t, d), F32),
        scratch_shapes=[pltpu.VMEM((tb, nq), MXU_DTYPE)],
        compiler_params=_cparams("parallel", "parallel"),
        name="swa_attention",
    )(sinks, q, k, k, v, v, x2d, w_out.astype(MXU_DTYPE), mod)


def kernel(x, c, positions, mod_w, mod_b, norm_g, gla_w_in, gla_w_g2, gla_b_g2, gla_norm_g, gla_w_out,
           kv_mod_w, kv_mod_b, kv_norm_g, kv_w, swa_w_q, swa_sinks, swa_w_out, peer_w_q, peer_subkeys,
           peer_u, peer_v, final_norm_g):
    bsz, seq, d = x.shape
    x2d = x.reshape(bsz * seq, d)
    mod0 = _mod_vectors(c, mod_w[0], mod_b[0]).reshape(bsz, 1, 6 * d)
    mod1 = _mod_vectors(c, mod_w[1], mod_b[1]).reshape(bsz, 1, 6 * d)
    kv_mod = _mod_vectors(c, kv_mod_w, kv_mod_b).reshape(bsz, 1, 2 * d)
    rope = _rope_tables(positions)

    qkvo, la = _gla_in(x2d, seq, norm_g[0, 0], mod0, gla_w_in[0], gla_w_g2[0], gla_b_g2[0])
    x2d = _gla_core(x2d, bsz, seq, qkvo, la, gla_w_out[0], mod0, gla_norm_g[0])
    x2d = _peer(x2d, seq, norm_g[0, 1], mod0, peer_w_q[0], peer_subkeys[0], peer_u[0], peer_v[0],
                final_norm_g, False)

    k, v, q = _kvq(x2d, seq, kv_norm_g, kv_mod, kv_w, norm_g[1, 0], mod1, swa_w_q[0], rope)
    x2d = _swa(x2d, bsz, seq, q, k, v, swa_sinks[0], swa_w_out[0], mod1)
    x2d = _peer(x2d, seq, norm_g[1, 1], mod1, peer_w_q[1], peer_subkeys[1], peer_u[1], peer_v[1],
                final_norm_g, True)
    return x2d.reshape(bsz, seq, d)
```

```python
import functools

import jax
import jax.numpy as jnp
from jax import lax
from jax.experimental import pallas as pl
from jax.experimental.pallas import tpu as pltpu

F32 = jnp.float32
MXU_DTYPE = jnp.bfloat16
EPS = 1e-6
GELU_C = 0.7071067811865476
NEG_INF = float("-inf")
POS_INF = float("inf")

LANES = 128
VMEM_LIMIT_BYTES = 56 * 1024 * 1024

GLA_HEADS, GLA_DK, GLA_DV, GLA_RANK, GLA_TAU, GLA_CHUNK = 4, 128, 256, 16, 16.0, 64
SWA_HEADS, SWA_KV_HEADS, SWA_HD, SWA_WINDOW = 16, 4, 64, 128
ROT_DIM, ROPE_THETA = 16, 500000.0
PEER_HEADS, PEER_KEYS, PEER_TOPK, PEER_QDIM = 8, 128, 16, 256

TB_PROJ = 512
TB_GLA = 256
TB_SWA = 512
TB_PEER = 512
TB_ROUTE = 256
EC_PEER = 1024

_NT = (((1,), (1,)), ((), ()))
_TN = (((0,), (0,)), ((), ()))


def _cparams(*sem):
    return pltpu.CompilerParams(dimension_semantics=sem, vmem_limit_bytes=VMEM_LIMIT_BYTES)


def _rms_modulate(x, g, shift, scale):
    y = x * lax.rsqrt(jnp.mean(x * x, axis=-1, keepdims=True) + EPS) * g
    return y * (1.0 + scale) + shift


def _dot(a, b):
    return jnp.dot(a.astype(MXU_DTYPE), b.astype(MXU_DTYPE), preferred_element_type=F32)


def _dot_nt(a, b):
    return lax.dot_general(a.astype(MXU_DTYPE), b.astype(MXU_DTYPE), _NT, preferred_element_type=F32)


def _mod_kernel(c_ref, w_ref, b_ref, o_ref):
    c = c_ref[...]
    ca = c * jax.nn.sigmoid(c)
    o_ref[...] = jnp.dot(ca, w_ref[...], preferred_element_type=F32,
                         precision=lax.Precision.HIGHEST) + b_ref[...]


def _mod_vectors(c, w, b):
    bsz, d = c.shape
    n = w.shape[1]
    bn = 1024
    return pl.pallas_call(
        _mod_kernel,
        grid=(n // bn,),
        in_specs=[pl.BlockSpec((bsz, d), lambda j: (0, 0)),
                  pl.BlockSpec((d, bn), lambda j: (0, j)),
                  pl.BlockSpec((1, bn), lambda j: (0, j))],
        out_specs=pl.BlockSpec((bsz, bn), lambda j: (0, j)),
        out_shape=jax.ShapeDtypeStruct((bsz, n), F32),
        compiler_params=_cparams("parallel"),
        name="mod_vectors",
    )(c, w, b.reshape(1, n))


def _rope_kernel(pos_ref, inv_ref, cos_ref, sinm_ref, sinp_ref):
    ang = pos_ref[...].astype(F32) * inv_ref[...]
    lane = lax.broadcasted_iota(jnp.int32, ang.shape, 1) % SWA_HD
    lo = lane < ROT_DIM // 2
    hi = (lane >= ROT_DIM // 2) & (lane < ROT_DIM)
    cs, sn = jnp.cos(ang), jnp.sin(ang)
    cos_ref[...] = jnp.where(lo | hi, cs, 1.0)
    sinm_ref[...] = jnp.where(lo, -sn, 0.0)
    sinp_ref[...] = jnp.where(hi, sn, 0.0)


def _rope_tables(positions):
    t = positions.size
    half = ROT_DIM // 2
    inv = ROPE_THETA ** (-jnp.arange(0, ROT_DIM, 2, dtype=F32) / ROT_DIM)
    inv_row = jnp.tile(inv, LANES // half).reshape(1, LANES)
    tb = TB_PROJ
    spec = pl.BlockSpec((tb, LANES), lambda i: (i, 0))
    return pl.pallas_call(
        _rope_kernel,
        grid=(t // tb,),
        in_specs=[pl.BlockSpec((tb, 1), lambda i: (i, 0)), pl.BlockSpec((1, LANES), lambda i: (0, 0))],
        out_specs=[spec, spec, spec],
        out_shape=[jax.ShapeDtypeStruct((t, LANES), F32)] * 3,
        compiler_params=_cparams("parallel"),
        name="rope_tables",
    )(positions.reshape(t, 1), inv_row)


def _apply_rope(x, cos, sinm, sinp):
    outs = []
    for g in range(x.shape[1] // LANES):
        xs = x[:, g * LANES:(g + 1) * LANES]
        outs.append(xs * cos + pltpu.roll(xs, LANES - ROT_DIM // 2, 1) * sinm
                    + pltpu.roll(xs, ROT_DIM // 2, 1) * sinp)
    return jnp.concatenate(outs, axis=1)


def _gla_in_kernel(x_ref, g_ref, sh_ref, sc_ref, wm_ref, wgl_ref, wg2_ref, bg2_ref, qkvo_ref, la_ref):
    h = _rms_modulate(x_ref[...], g_ref[...], sh_ref[0], sc_ref[0]).astype(MXU_DTYPE)
    qkvo_ref[...] = jnp.dot(h, wm_ref[...], preferred_element_type=F32).astype(qkvo_ref.dtype)
    gl = jnp.dot(h, wgl_ref[...], preferred_element_type=F32)
    z = jnp.dot(gl.astype(MXU_DTYPE), wg2_ref[...], preferred_element_type=F32) + bg2_ref[...]
    la_ref[...] = (jnp.minimum(z, 0.0) - jnp.log1p(jnp.exp(-jnp.abs(z)))) * (1.0 / GLA_TAU)


def _gla_in(x2d, seq, g, mod, w_in, w_g2, b_g2):
    t, d = x2d.shape
    tb = TB_PROJ
    spb = seq // tb
    n_main = GLA_HEADS * (2 * GLA_DK + 2 * GLA_DV)
    n_gate = GLA_HEADS * GLA_DK
    w_main = w_in[:, :n_main].astype(MXU_DTYPE)
    w_gl = jnp.pad(w_in[:, n_main:], ((0, 0), (0, LANES - GLA_RANK))).astype(MXU_DTYPE)
    w_g2p = jnp.pad(w_g2, ((0, LANES - GLA_RANK), (0, 0))).astype(MXU_DTYPE)
    const = lambda i: (0, 0)
    return pl.pallas_call(
        _gla_in_kernel,
        grid=(t // tb,),
        in_specs=[pl.BlockSpec((tb, d), lambda i: (i, 0)),
                  pl.BlockSpec((1, d), const),
                  pl.BlockSpec((1, 1, d), lambda i: (i // spb, 0, 0)),
                  pl.BlockSpec((1, 1, d), lambda i: (i // spb, 0, 1)),
                  pl.BlockSpec((d, n_main), const),
                  pl.BlockSpec((d, LANES), const),
                  pl.BlockSpec((LANES, n_gate), const),
                  pl.BlockSpec((1, n_gate), const)],
        out_specs=[pl.BlockSpec((tb, n_main), lambda i: (i, 0)),
                   pl.BlockSpec((tb, n_gate), lambda i: (i, 0))],
        out_shape=[jax.ShapeDtypeStruct((t, n_main), MXU_DTYPE),
                   jax.ShapeDtypeStruct((t, n_gate), F32)],
        compiler_params=_cparams("parallel"),
        name="gla_in_proj",
    )(x2d, g.reshape(1, d), mod, mod, w_main, w_gl, w_g2p, b_g2.reshape(1, n_gate))


def _gla_core_kernel(qkvo_ref, la_ref, x_ref, wout_ref, gate_ref, ng_ref, o_ref, st_ref, gated_ref):
    @pl.when(pl.program_id(1) == 0)
    def _():
        st_ref[...] = jnp.zeros_like(st_ref)

    c_len, dk, dv = GLA_CHUNK, GLA_DK, GLA_DV
    k0, v0, g0 = GLA_HEADS * dk, 2 * GLA_HEADS * dk, 2 * GLA_HEADS * dk + GLA_HEADS * dv
    tb = x_ref.shape[0]
    n_c = tb // c_len
    row = lax.broadcasted_iota(jnp.int32, (tb, tb), 0)
    col = lax.broadcasted_iota(jnp.int32, (tb, tb), 1)
    tri = (row // c_len == col // c_len) & (row >= col)
    b_all = jnp.dot(tri.astype(F32), la_ref[...], preferred_element_type=F32,
                    precision=lax.Precision.HIGHEST)
    for h in range(GLA_HEADS):
        b = b_all[:, h * dk:(h + 1) * dk]
        lasts = [b[(c + 1) * c_len - 1:(c + 1) * c_len, :] for c in range(n_c)]
        b_last = jnp.concatenate([jnp.broadcast_to(bl, (c_len, dk)) for bl in lasts], axis=0)
        q = qkvo_ref[:, h * dk:(h + 1) * dk].astype(F32) * (dk ** -0.5)
        k = qkvo_ref[:, k0 + h * dk:k0 + (h + 1) * dk].astype(F32)
        v = qkvo_ref[:, v0 + h * dv:v0 + (h + 1) * dv]
        og = qkvo_ref[:, g0 + h * dv:g0 + (h + 1) * dv].astype(F32)
        q_t = (q * jnp.exp(b)).astype(MXU_DTYPE)
        k_t = k * jnp.exp(-b)
        k_dec = (k * jnp.exp(b_last - b)).astype(MXU_DTYPE)
        scores = jnp.where(tri, _dot_nt(q_t, k_t), 0.0)
        o_intra = _dot(scores, v)
        st = st_ref[h]
        o_inter = []
        for c in range(n_c):
            rs = slice(c * c_len, (c + 1) * c_len)
            o_inter.append(_dot_nt(q_t[rs], st))
            st = jnp.exp(lasts[c]) * st + lax.dot_general(v[rs], k_dec[rs], _TN, preferred_element_type=F32)
        st_ref[h] = st
        o = o_intra + jnp.concatenate(o_inter, axis=0)
        o = o * lax.rsqrt(jnp.mean(o * o, axis=-1, keepdims=True) + EPS) * ng_ref[...]
        o = o * (og * jax.nn.sigmoid(og))
        gated_ref[:, h * dv:(h + 1) * dv] = o.astype(gated_ref.dtype)
    y = jnp.dot(gated_ref[...], wout_ref[...], preferred_element_type=F32)
    o_ref[...] = x_ref[...] + gate_ref[0] * y


def _gla_core(x2d, bsz, seq, qkvo, la, w_out, mod, norm_g):
    t, d = x2d.shape
    tb = TB_GLA
    spb = seq // tb
    n_main, n_gate = qkvo.shape[1], la.shape[1]
    hv = GLA_HEADS * GLA_DV
    tok = lambda b, s: (b * spb + s, 0)
    return pl.pallas_call(
        _gla_core_kernel,
        grid=(bsz, spb),
        in_specs=[pl.BlockSpec((tb, n_main), tok),
                  pl.BlockSpec((tb, n_gate), tok),
                  pl.BlockSpec((tb, d), tok),
                  pl.BlockSpec((hv, d), lambda b, s: (0, 0)),
                  pl.BlockSpec((1, 1, d), lambda b, s: (b, 0, 2)),
                  pl.BlockSpec((1, GLA_DV), lambda b, s: (0, 0))],
        out_specs=pl.BlockSpec((tb, d), tok),
        out_shape=jax.ShapeDtypeStruct((t, d), F32),
        scratch_shapes=[pltpu.VMEM((GLA_HEADS, GLA_DV, GLA_DK), F32),
                        pltpu.VMEM((tb, hv), MXU_DTYPE)],
        compiler_params=_cparams("parallel", "arbitrary"),
        name="gla_core",
    )(qkvo, la, x2d, w_out.astype(MXU_DTYPE), mod, norm_g.reshape(1, GLA_DV))


def _sort16_pairs():
    def merge(lo, hi, r):
        step = r * 2
        if step < hi - lo:
            yield from merge(lo, hi, step)
            yield from merge(lo + r, hi, step)
            yield from [(i, i + r) for i in range(lo + r, hi - r, step)]
        else:
            yield (lo, lo + r)

    def sort(lo, hi):
        if hi - lo >= 1:
            mid = lo + (hi - lo) // 2
            yield from sort(lo, mid)
            yield from sort(mid + 1, hi)
            yield from merge(lo, hi, 1)

    return tuple(sort(0, PEER_TOPK - 1))


_SORT16 = _sort16_pairs()


def _sort16_desc(w):
    w = list(w)
    for i, j in _SORT16:
        w[i], w[j] = jnp.maximum(w[i], w[j]), jnp.minimum(w[i], w[j])
    return w


def _merge_top16(a, b):
    n = PEER_TOPK
    w = [jnp.maximum(a[i], b[n - 1 - i]) for i in range(n)]
    d = n // 2
    while d >= 1:
        for i in range(n):
            if i & d == 0:
                w[i], w[i + d] = jnp.maximum(w[i], w[i + d]), jnp.minimum(w[i], w[i + d])
        d //= 2
    return w


def _top16_desc(wires):
    n = PEER_TOPK
    runs = [_sort16_desc(wires[i:i + n]) for i in range(0, len(wires), n)]
    while len(runs) > 1:
        runs = [_merge_top16(runs[i], runs[i + 1]) for i in range(0, len(runs), 2)]
    return runs[0]


def _route_tile(s1, s2):
    n = PEER_TOPK
    v1, v2 = _top16_desc(s1), _top16_desc(s2)
    run = [v1[0] + v2[b] for b in range(n)]
    for a in range(1, n):
        run = _merge_top16(run, [v1[a] + v2[b] for b in range(n)])
    tau = run[n - 1]
    e1 = [jnp.exp(v - v1[0]) for v in v1]
    e2 = [jnp.exp(v - v2[0]) for v in v2]
    z = jnp.zeros_like(tau)
    thr_rank = []
    for a in range(n):
        za = jnp.zeros_like(tau)
        ta = jnp.full_like(tau, POS_INF)
        for b in range(n):
            sel = (v1[a] + v2[b]) >= tau
            za = za + jnp.where(sel, e2[b], 0.0)
            ta = jnp.where(sel, v2[b], ta)
        z = z + e1[a] * za
        thr_rank.append(ta)
    rz = 1.0 / z
    thr, coef = [], []
    for s in s1:
        t = jnp.full_like(tau, POS_INF)
        for a in range(n - 1, -1, -1):
            t = jnp.where(s == v1[a], thr_rank[a], t)
        thr.append(t)
        coef.append(jnp.exp(s - v1[0]) * rz * GELU_C)
    return thr, coef, v2[0]


def _peer_route_kernel(x_ref, g_ref, sh_ref, sc_ref, wqt_ref, k1i_ref, k2i_ref, k2n_ref,
                       h_ref, thr_ref, coef_ref, s2_ref, e2_ref):
    nk, nh = PEER_KEYS, PEER_HEADS
    h = _rms_modulate(x_ref[...], g_ref[...], sh_ref[0], sc_ref[0]).astype(MXU_DTYPE)
    h_ref[...] = h
    q_t = lax.dot_general(wqt_ref[...], h, _NT, preferred_element_type=F32).astype(MXU_DTYPE)
    half = q_t.shape[0] // 2
    s1i = jnp.dot(k1i_ref[...], q_t[:half], preferred_element_type=F32)
    s2i = jnp.dot(k2i_ref[...], q_t[half:], preferred_element_type=F32)
    s2n = jnp.dot(k2n_ref[...], q_t[half:], preferred_element_type=F32)
    for lt in range(thr_ref.shape[0]):
        ls = slice(lt * LANES, (lt + 1) * LANES)
        w1 = [s1i[k * nh:(k + 1) * nh, ls] for k in range(nk)]
        w2 = [s2i[k * nh:(k + 1) * nh, ls] for k in range(nk)]
        thr, coef, m2 = _route_tile(w1, w2)
        for k in range(nk):
            thr_ref[lt, k] = thr[k]
            coef_ref[lt, k] = coef[k]
        for hh in range(nh):
            s2h = s2n[hh * nk:(hh + 1) * nk, ls]
            s2_ref[lt, hh] = s2h
            e2_ref[lt, hh] = jnp.exp(s2h - m2[hh:hh + 1])


def _peer_route(x2d, seq, g, mod, w_q, subkeys):
    t, d = x2d.shape
    tb = TB_ROUTE
    spb = seq // tb
    nh, nk, half = PEER_HEADS, PEER_KEYS, PEER_QDIM // 2
    nq = nh * half
    wqt = w_q.reshape(d, nh, 2, half).transpose(2, 1, 3, 0).reshape(2 * nq, d).astype(MXU_DTYPE)
    eye = jnp.eye(nh, dtype=subkeys.dtype)
    interleaved = lambda k: (k[:, None, None, :] * eye[None, :, :, None]).reshape(nk * nh, nq).astype(MXU_DTYPE)
    natural = lambda k: (eye[:, None, :, None] * k[None, :, None, :]).reshape(nh * nk, nq).astype(MXU_DTYPE)
    const2 = lambda i: (0, 0)
    tiles = tb // LANES
    side_k = pl.BlockSpec((tiles, nk, nh, LANES), lambda i: (i, 0, 0, 0))
    side_h = pl.BlockSpec((tiles, nh, nk, LANES), lambda i: (i, 0, 0, 0))
    shape_k = jax.ShapeDtypeStruct((t // LANES, nk, nh, LANES), F32)
    shape_h = jax.ShapeDtypeStruct((t // LANES, nh, nk, LANES), F32)
    return pl.pallas_call(
        _peer_route_kernel,
        grid=(t // tb,),
        in_specs=[pl.BlockSpec((tb, d), lambda i: (i, 0)),
                  pl.BlockSpec((1, d), const2),
                  pl.BlockSpec((1, 1, d), lambda i: (i // spb, 0, 3)),
                  pl.BlockSpec((1, 1, d), lambda i: (i // spb, 0, 4)),
                  pl.BlockSpec((2 * nq, d), const2),
                  pl.BlockSpec((nk * nh, nq), const2),
                  pl.BlockSpec((nk * nh, nq), const2),
                  pl.BlockSpec((nh * nk, nq), const2)],
        out_specs=[pl.BlockSpec((tb, d), lambda i: (i, 0)), side_k, side_k, side_h, side_h],
        out_shape=[jax.ShapeDtypeStruct((t, d), MXU_DTYPE), shape_k, shape_k, shape_h, shape_h],
        compiler_params=_cparams("parallel"),
        name="peer_route",
    )(x2d, g.reshape(1, d), mod, mod, wqt, interleaved(subkeys[0]), interleaved(subkeys[1]),
      natural(subkeys[1]))


def _gelu_scaled(xs):
    return xs * (1.0 + lax.erf(xs))


def _peer_dense_kernel(h_ref, u_ref, vt_ref, thr_ref, coef_ref, s2_ref, e2_ref, x_ref, gate_ref, fg_ref,
                       o_ref, st_ref, w_ref, acc_ref, *, final_norm):
    c = pl.program_id(1)
    ec, tb = st_ref.shape
    groups = ec // PEER_KEYS

    @pl.when(c == 0)
    def _():
        acc_ref[...] = jnp.zeros_like(acc_ref)

    st_ref[...] = lax.dot_general(u_ref[...], h_ref[...], _NT, preferred_element_type=F32)

    def group_body(j, carry):
        i1 = c * groups + j
        r0 = pl.multiple_of(j * PEER_KEYS, PEER_KEYS)
        for lt in range(tb // LANES):
            ls = slice(lt * LANES, (lt + 1) * LANES)
            gw = jnp.zeros((PEER_KEYS, LANES), F32)
            for hh in range(PEER_HEADS):
                thr = thr_ref[lt, i1, pl.ds(hh, 8, stride=0), :]
                cf = coef_ref[lt, i1, pl.ds(hh, 8, stride=0), :]
                thr = jnp.concatenate([thr] * (PEER_KEYS // 8), axis=0)
                cf = jnp.concatenate([cf] * (PEER_KEYS // 8), axis=0)
                gw = gw + jnp.where(s2_ref[lt, hh] >= thr, e2_ref[lt, hh] * cf, 0.0)
            a = _gelu_scaled(st_ref[pl.ds(r0, PEER_KEYS), ls])
            w_ref[pl.ds(r0, PEER_KEYS), ls] = (a * gw).astype(w_ref.dtype)
        return carry

    lax.fori_loop(0, groups, group_body, 0)
    acc_ref[...] += jnp.dot(vt_ref[...], w_ref[...], preferred_element_type=F32)

    @pl.when(c == pl.num_programs(1) - 1)
    def _():
        y = x_ref[...] + gate_ref[0] * acc_ref[...].T
        if final_norm:
            y = y * lax.rsqrt(jnp.mean(y * y, axis=-1, keepdims=True) + EPS) * fg_ref[...]
        o_ref[...] = y


def _peer_dense(x2d, seq, h, side, mod, u_tab, v_tab, final_g, final_norm):
    t, d = x2d.shape
    tb, ec = TB_PEER, EC_PEER
    spb = seq // tb
    n_exp = u_tab.shape[0]
    u = (u_tab * GELU_C).astype(MXU_DTYPE)
    vt = v_tab.T.astype(MXU_DTYPE)
    thr, coef, s2, e2 = side
    tiles = tb // LANES
    side_k = pl.BlockSpec((tiles, PEER_KEYS, PEER_HEADS, LANES), lambda i, c: (i, 0, 0, 0))
    side_h = pl.BlockSpec((tiles, PEER_HEADS, PEER_KEYS, LANES), lambda i, c: (i, 0, 0, 0))
    return pl.pallas_call(
        functools.partial(_peer_dense_kernel, final_norm=final_norm),
        grid=(t // tb, n_exp // ec),
        in_specs=[pl.BlockSpec((tb, d), lambda i, c: (i, 0)),
                  pl.BlockSpec((ec, d), lambda i, c: (c, 0)),
                  pl.BlockSpec((d, ec), lambda i, c: (0, c)),
                  side_k, side_k, side_h, side_h,
                  pl.BlockSpec((tb, d), lambda i, c: (i, 0)),
                  pl.BlockSpec((1, 1, d), lambda i, c: (i // spb, 0, 5)),
                  pl.BlockSpec((1, d), lambda i, c: (0, 0))],
        out_specs=pl.BlockSpec((tb, d), lambda i, c: (i, 0)),
        out_shape=jax.ShapeDtypeStruct((t, d), F32),
        scratch_shapes=[pltpu.VMEM((ec, tb), F32),
                        pltpu.VMEM((ec, tb), MXU_DTYPE),
                        pltpu.VMEM((d, tb), F32)],
        compiler_params=_cparams("parallel", "arbitrary"),
        name="peer_dense",
    )(h, u, vt, thr, coef, s2, e2, x2d, mod, final_g.reshape(1, d))


def _peer(x2d, seq, g, mod, w_q, subkeys, u_tab, v_tab, final_g, final_norm):
    h, *side = _peer_route(x2d, seq, g, mod, w_q, subkeys)
    return _peer_dense(x2d, seq, h, side, mod, u_tab, v_tab, final_g, final_norm)


def _kvq_kernel(x_ref, gkv_ref, shkv_ref, sckv_ref, gq_ref, shq_ref, scq_ref, wkv_ref, wq_ref,
                cos_ref, sinm_ref, sinp_ref, k_ref, v_ref, q_ref):
    x = x_ref[...]
    cos, sinm, sinp = cos_ref[...], sinm_ref[...], sinp_ref[...]
    hk = _rms_modulate(x, gkv_ref[...], shkv_ref[0], sckv_ref[0])
    kv = _dot(hk, wkv_ref[...])
    nk = k_ref.shape[1]
    k_ref[...] = _apply_rope(kv[:, :nk], cos, sinm, sinp).astype(k_ref.dtype)
    v_ref[...] = kv[:, nk:].astype(v_ref.dtype)
    hq = _rms_modulate(x, gq_ref[...], shq_ref[0], scq_ref[0])
    q_ref[...] = (_apply_rope(_dot(hq, wq_ref[...]), cos, sinm, sinp) * (SWA_HD ** -0.5)).astype(q_ref.dtype)


def _kvq(x2d, seq, kv_norm_g, kv_mod, kv_w, q_norm_g, mod, w_q, rope):
    t, d = x2d.shape
    tb = TB_PROJ
    spb = seq // tb
    nk = SWA_KV_HEADS * SWA_HD
    nq = SWA_HEADS * SWA_HD
    const = lambda i: (0, 0)
    tok = lambda w: pl.BlockSpec((tb, w), lambda i: (i, 0))
    modspec = lambda k: pl.BlockSpec((1, 1, d), lambda i: (i // spb, 0, k))
    return pl.pallas_call(
        _kvq_kernel,
        grid=(t // tb,),
        in_specs=[tok(d),
                  pl.BlockSpec((1, d), const), modspec(0), modspec(1),
                  pl.BlockSpec((1, d), const), modspec(0), modspec(1),
                  pl.BlockSpec((d, 2 * nk), const), pl.BlockSpec((d, nq), const),
                  tok(LANES), tok(LANES), tok(LANES)],
        out_specs=[tok(nk), tok(nk), tok(nq)],
        out_shape=[jax.ShapeDtypeStruct((t, nk), MXU_DTYPE), jax.ShapeDtypeStruct((t, nk), MXU_DTYPE),
                   jax.ShapeDtypeStruct((t, nq), MXU_DTYPE)],
        compiler_params=_cparams("parallel"),
        name="kv_q_proj",
    )(x2d, kv_norm_g.reshape(1, d), kv_mod, kv_mod, q_norm_g.reshape(1, d), mod, mod,
      kv_w.astype(MXU_DTYPE), w_q.astype(MXU_DTYPE), *rope)


def _swa_kernel(sink_ref, q_ref, kc_ref, kp_ref, vc_ref, vp_ref, x_ref, wout_ref, gate_ref,
                o_ref, att_ref):
    p = SWA_WINDOW
    group = SWA_HEADS // SWA_KV_HEADS
    first = pl.program_id(1) == 0
    lane = lax.broadcasted_iota(jnp.int32, (p, LANES), 1)
    rows = 2 * p
    qi = lax.broadcasted_iota(jnp.int32, (rows, 2 * p), 0) % p
    mi = lax.broadcasted_iota(jnp.int32, (rows, 2 * p), 1)
    band = (mi > qi) & (mi <= qi + p)
    top_half = lax.broadcasted_iota(jnp.int32, (rows, 1), 0) < p
    zero = jnp.zeros((), MXU_DTYPE)
    for jb in range(q_ref.shape[0] // p):
        rs = slice(jb * p, (jb + 1) * p)
        if jb == 0:
            k_prev, v_prev = kp_ref[...], vp_ref[...]
            mask = band & ((mi >= p) | jnp.logical_not(first))
        else:
            ps = slice((jb - 1) * p, jb * p)
            k_prev, v_prev = kc_ref[ps, :], vc_ref[ps, :]
            mask = band
        k_band = jnp.concatenate([k_prev, kc_ref[rs, :]], axis=0)
        v_band = jnp.concatenate([v_prev, vc_ref[rs, :]], axis=0)
        for kh in range(SWA_KV_HEADS):
            kv_tile = k_band[:, (kh // 2) * LANES:(kh // 2 + 1) * LANES]
            vv_tile = v_band[:, (kh // 2) * LANES:(kh // 2 + 1) * LANES]
            mine = (lane[0:1] // SWA_HD) == (kh % 2)
            k_own = jnp.where(mine, kv_tile, zero)
            v_own = jnp.where(mine, vv_tile, zero)
            k_swp = pltpu.roll(k_own.astype(F32), SWA_HD, 1).astype(MXU_DTYPE)
            v_swp = pltpu.roll(v_own.astype(F32), SWA_HD, 1).astype(MXU_DTYPE)
            k_lo, k_hi = (k_own, k_swp) if kh % 2 == 0 else (k_swp, k_own)
            v_lo, v_hi = (v_own, v_swp) if kh % 2 == 0 else (v_swp, v_own)
            t0 = kh * group // 2
            q2 = jnp.concatenate([q_ref[rs, t0 * LANES:(t0 + 1) * LANES],
                                  q_ref[rs, (t0 + 1) * LANES:(t0 + 2) * LANES]], axis=0)
            out = jnp.zeros((rows, LANES), F32)
            for pos, (k_x, v_x) in enumerate(((k_lo, v_lo), (k_hi, v_hi))):
                s = jnp.where(mask, _dot_nt(q2, k_x), NEG_INF)
                sink = jnp.where(top_half, sink_ref[kh * group + pos], sink_ref[kh * group + 2 + pos])
                m = jnp.maximum(jnp.max(s, axis=-1, keepdims=True), sink)
                e = jnp.exp(s - m)
                denom = jnp.sum(e, axis=-1, keepdims=True) + jnp.exp(sink - m)
                out = out + _dot(e, v_x) * (1.0 / denom)
            att_ref[rs, t0 * LANES:(t0 + 1) * LANES] = out[:p].astype(att_ref.dtype)
            att_ref[rs, (t0 + 1) * LANES:(t0 + 2) * LANES] = out[p:].astype(att_ref.dtype)
    y = jnp.dot(att_ref[...], wout_ref[...], preferred_element_type=F32)
    o_ref[...] = x_ref[...] + gate_ref[0] * y


def _swa(x2d, bsz, seq, q, k, v, sinks, w_out, mod):
    t, d = x2d.shape
    tb, p = TB_SWA, SWA_WINDOW
    spb = seq // tb
    nk, nq = k.shape[1], q.shape[1]
    tok = lambda b, s: (b * spb + s, 0)
    prev = lambda b, s: (jnp.maximum((b * spb + s) * (tb // p) - 1, 0), 0)
    return pl.pallas_call(
        _swa_kernel,
        grid=(bsz, spb),
        in_specs=[pl.BlockSpec(memory_space=pltpu.SMEM),
                  pl.BlockSpec((tb, nq), tok),
                  pl.BlockSpec((tb, nk), tok), pl.BlockSpec((p, nk), prev),
                  pl.BlockSpec((tb, nk), tok), pl.BlockSpec((p, nk), prev),
                  pl.BlockSpec((tb, d), tok),
                  pl.BlockSpec((nq, d), lambda b, s: (0, 0)),
                  pl.BlockSpec((1, 1, d), lambda b, s: (b, 0, 2))],
        out_specs=pl.BlockSpec((tb, d), tok),
        out_shape=jax.ShapeDtypeStruct((t, d), F32),
        scratch_shapes=[pltpu.VMEM((tb, nq), MXU_DTYPE)],
        compiler_params=_cparams("parallel", "parallel"),
        name="swa_attention",
    )(sinks, q, k, k, v, v, x2d, w_out.astype(MXU_DTYPE), mod)


def kernel(x, c, positions, mod_w, mod_b, norm_g, gla_w_in, gla_w_g2, gla_b_g2, gla_norm_g, gla_w_out,
           kv_mod_w, kv_mod_b, kv_norm_g, kv_w, swa_w_q, swa_sinks, swa_w_out, peer_w_q, peer_subkeys,
           peer_u, peer_v, final_norm_g):
    bsz, seq, d = x.shape
    x2d = x.reshape(bsz * seq, d)
    mod0 = _mod_vectors(c, mod_w[0], mod_b[0]).reshape(bsz, 1, 6 * d)
    mod1 = _mod_vectors(c, mod_w[1], mod_b[1]).reshape(bsz, 1, 6 * d)
    kv_mod = _mod_vectors(c, kv_mod_w, kv_mod_b).reshape(bsz, 1, 2 * d)
    rope = _rope_tables(positions)

    qkvo, la = _gla_in(x2d, seq, norm_g[0, 0], mod0, gla_w_in[0], gla_w_g2[0], gla_b_g2[0])
    x2d = _gla_core(x2d, bsz, seq, qkvo, la, gla_w_out[0], mod0, gla_norm_g[0])
    x2d = _peer(x2d, seq, norm_g[0, 1], mod0, peer_w_q[0], peer_subkeys[0], peer_u[0], peer_v[0],
                final_norm_g, False)

    k, v, q = _kvq(x2d, seq, kv_norm_g, kv_mod, kv_w, norm_g[1, 0], mod1, swa_w_q[0], rope)
    x2d = _swa(x2d, bsz, seq, q, k, v, swa_sinks[0], swa_w_out[0], mod1)
    x2d = _peer(x2d, seq, norm_g[1, 1], mod1, peer_w_q[1], peer_subkeys[1], peer_u[1], peer_v[1],
                final_norm_g, True)
    return x2d.reshape(bsz, seq, d)
```

```python
import functools

import jax
import jax.numpy as jnp
from jax import lax
from jax.experimental import pallas as pl
from jax.experimental.pallas import tpu as pltpu

F32 = jnp.float32
MXU_DTYPE = jnp.bfloat16
EPS = 1e-6
GELU_C = 0.7071067811865476
NEG_INF = float("-inf")
POS_INF = float("inf")

LANES = 128
VMEM_LIMIT_BYTES = 56 * 1024 * 1024

GLA_HEADS, GLA_DK, GLA_DV, GLA_RANK, GLA_TAU, GLA_CHUNK = 4, 128, 256, 16, 16.0, 64
SWA_HEADS, SWA_KV_HEADS, SWA_HD, SWA_WINDOW = 16, 4, 64, 128
ROT_DIM, ROPE_THETA = 16, 500000.0
PEER_HEADS, PEER_KEYS, PEER_TOPK, PEER_QDIM = 8, 128, 16, 256

TB_PROJ = 512
TB_GLA = 256
TB_SWA = 512
TB_PEER = 512
TB_ROUTE = 512
EC_PEER = 1024

_NT = (((1,), (1,)), ((), ()))
_TN = (((0,), (0,)), ((), ()))


def _cparams(*sem):
    return pltpu.CompilerParams(dimension_semantics=sem, vmem_limit_bytes=VMEM_LIMIT_BYTES)


def _rms_modulate(x, g, shift, scale):
    y = x * lax.rsqrt(jnp.mean(x * x, axis=-1, keepdims=True) + EPS) * g
    return y * (1.0 + scale) + shift


def _dot(a, b):
    return jnp.dot(a.astype(MXU_DTYPE), b.astype(MXU_DTYPE), preferred_element_type=F32)


def _dot_nt(a, b):
    return lax.dot_general(a.astype(MXU_DTYPE), b.astype(MXU_DTYPE), _NT, preferred_element_type=F32)


def _mod_kernel(c_ref, w_ref, b_ref, o_ref):
    c = c_ref[...]
    ca = c * jax.nn.sigmoid(c)
    o_ref[...] = jnp.dot(ca, w_ref[...], preferred_element_type=F32,
                         precision=lax.Precision.HIGHEST) + b_ref[...]


def _mod_vectors(c, w, b):
    bsz, d = c.shape
    n = w.shape[1]
    bn = 1024
    return pl.pallas_call(
        _mod_kernel,
        grid=(n // bn,),
        in_specs=[pl.BlockSpec((bsz, d), lambda j: (0, 0)),
                  pl.BlockSpec((d, bn), lambda j: (0, j)),
                  pl.BlockSpec((1, bn), lambda j: (0, j))],
        out_specs=pl.BlockSpec((bsz, bn), lambda j: (0, j)),
        out_shape=jax.ShapeDtypeStruct((bsz, n), F32),
        compiler_params=_cparams("parallel"),
        name="mod_vectors",
    )(c, w, b.reshape(1, n))


def _rope_kernel(pos_ref, inv_ref, cos_ref, sinm_ref, sinp_ref):
    ang = pos_ref[...].astype(F32) * inv_ref[...]
    lane = lax.broadcasted_iota(jnp.int32, ang.shape, 1) % SWA_HD
    lo = lane < ROT_DIM // 2
    hi = (lane >= ROT_DIM // 2) & (lane < ROT_DIM)
    cs, sn = jnp.cos(ang), jnp.sin(ang)
    cos_ref[...] = jnp.where(lo | hi, cs, 1.0)
    sinm_ref[...] = jnp.where(lo, -sn, 0.0)
    sinp_ref[...] = jnp.where(hi, sn, 0.0)


def _rope_tables(positions):
    t = positions.size
    half = ROT_DIM // 2
    inv = ROPE_THETA ** (-jnp.arange(0, ROT_DIM, 2, dtype=F32) / ROT_DIM)
    inv_row = jnp.tile(inv, LANES // half).reshape(1, LANES)
    tb = TB_PROJ
    spec = pl.BlockSpec((tb, LANES), lambda i: (i, 0))
    return pl.pallas_call(
        _rope_kernel,
        grid=(t // tb,),
        in_specs=[pl.BlockSpec((tb, 1), lambda i: (i, 0)), pl.BlockSpec((1, LANES), lambda i: (0, 0))],
        out_specs=[spec, spec, spec],
        out_shape=[jax.ShapeDtypeStruct((t, LANES), F32)] * 3,
        compiler_params=_cparams("parallel"),
        name="rope_tables",
    )(positions.reshape(t, 1), inv_row)


def _apply_rope(x, cos, sinm, sinp):
    outs = []
    for g in range(x.shape[1] // LANES):
        xs = x[:, g * LANES:(g + 1) * LANES]
        outs.append(xs * cos + pltpu.roll(xs, LANES - ROT_DIM // 2, 1) * sinm
                    + pltpu.roll(xs, ROT_DIM // 2, 1) * sinp)
    return jnp.concatenate(outs, axis=1)


def _gla_in_kernel(x_ref, g_ref, sh_ref, sc_ref, wm_ref, wgl_ref, wg2_ref, bg2_ref, qkvo_ref, la_ref):
    h = _rms_modulate(x_ref[...], g_ref[...], sh_ref[0], sc_ref[0]).astype(MXU_DTYPE)
    qkvo_ref[...] = jnp.dot(h, wm_ref[...], preferred_element_type=F32).astype(qkvo_ref.dtype)
    gl = jnp.dot(h, wgl_ref[...], preferred_element_type=F32)
    z = jnp.dot(gl.astype(MXU_DTYPE), wg2_ref[...], preferred_element_type=F32) + bg2_ref[...]
    la_ref[...] = (jnp.minimum(z, 0.0) - jnp.log1p(jnp.exp(-jnp.abs(z)))) * (1.0 / GLA_TAU)


def _gla_in(x2d, seq, g, mod, w_in, w_g2, b_g2):
    t, d = x2d.shape
    tb = TB_PROJ
    spb = seq // tb
    n_main = GLA_HEADS * (2 * GLA_DK + 2 * GLA_DV)
    n_gate = GLA_HEADS * GLA_DK
    w_main = w_in[:, :n_main].astype(MXU_DTYPE)
    w_gl = jnp.pad(w_in[:, n_main:], ((0, 0), (0, LANES - GLA_RANK))).astype(MXU_DTYPE)
    w_g2p = jnp.pad(w_g2, ((0, LANES - GLA_RANK), (0, 0))).astype(MXU_DTYPE)
    const = lambda i: (0, 0)
    return pl.pallas_call(
        _gla_in_kernel,
        grid=(t // tb,),
        in_specs=[pl.BlockSpec((tb, d), lambda i: (i, 0)),
                  pl.BlockSpec((1, d), const),
                  pl.BlockSpec((1, 1, d), lambda i: (i // spb, 0, 0)),
                  pl.BlockSpec((1, 1, d), lambda i: (i // spb, 0, 1)),
                  pl.BlockSpec((d, n_main), const),
                  pl.BlockSpec((d, LANES), const),
                  pl.BlockSpec((LANES, n_gate), const),
                  pl.BlockSpec((1, n_gate), const)],
        out_specs=[pl.BlockSpec((tb, n_main), lambda i: (i, 0)),
                   pl.BlockSpec((tb, n_gate), lambda i: (i, 0))],
        out_shape=[jax.ShapeDtypeStruct((t, n_main), MXU_DTYPE),
                   jax.ShapeDtypeStruct((t, n_gate), F32)],
        compiler_params=_cparams("parallel"),
        name="gla_in_proj",
    )(x2d, g.reshape(1, d), mod, mod, w_main, w_gl, w_g2p, b_g2.reshape(1, n_gate))


def _gla_core_kernel(qkvo_ref, la_ref, x_ref, wout_ref, gate_ref, ng_ref, o_ref, st_ref, gated_ref):
    @pl.when(pl.program_id(1) == 0)
    def _():
        st_ref[...] = jnp.zeros_like(st_ref)

    c_len, dk, dv = GLA_CHUNK, GLA_DK, GLA_DV
    k0, v0, g0 = GLA_HEADS * dk, 2 * GLA_HEADS * dk, 2 * GLA_HEADS * dk + GLA_HEADS * dv
    tb = x_ref.shape[0]
    n_c = tb // c_len
    row = lax.broadcasted_iota(jnp.int32, (tb, tb), 0)
    col = lax.broadcasted_iota(jnp.int32, (tb, tb), 1)
    tri = (row // c_len == col // c_len) & (row >= col)
    b_all = jnp.dot(tri.astype(F32), la_ref[...], preferred_element_type=F32,
                    precision=lax.Precision.HIGHEST)
    for h in range(GLA_HEADS):
        b = b_all[:, h * dk:(h + 1) * dk]
        lasts = [b[(c + 1) * c_len - 1:(c + 1) * c_len, :] for c in range(n_c)]
        b_last = jnp.concatenate([jnp.broadcast_to(bl, (c_len, dk)) for bl in lasts], axis=0)
        q = qkvo_ref[:, h * dk:(h + 1) * dk].astype(F32) * (dk ** -0.5)
        k = qkvo_ref[:, k0 + h * dk:k0 + (h + 1) * dk].astype(F32)
        v = qkvo_ref[:, v0 + h * dv:v0 + (h + 1) * dv]
        og = qkvo_ref[:, g0 + h * dv:g0 + (h + 1) * dv].astype(F32)
        q_t = (q * jnp.exp(b)).astype(MXU_DTYPE)
        k_t = k * jnp.exp(-b)
        k_dec = (k * jnp.exp(b_last - b)).astype(MXU_DTYPE)
        scores = jnp.where(tri, _dot_nt(q_t, k_t), 0.0)
        o_intra = _dot(scores, v)
        st = st_ref[h]
        o_inter = []
        for c in range(n_c):
            rs = slice(c * c_len, (c + 1) * c_len)
            o_inter.append(_dot_nt(q_t[rs], st))
            st = jnp.exp(lasts[c]) * st + lax.dot_general(v[rs], k_dec[rs], _TN, preferred_element_type=F32)
        st_ref[h] = st
        o = o_intra + jnp.concatenate(o_inter, axis=0)
        o = o * lax.rsqrt(jnp.mean(o * o, axis=-1, keepdims=True) + EPS) * ng_ref[...]
        o = o * (og * jax.nn.sigmoid(og))
        gated_ref[:, h * dv:(h + 1) * dv] = o.astype(gated_ref.dtype)
    y = jnp.dot(gated_ref[...], wout_ref[...], preferred_element_type=F32)
    o_ref[...] = x_ref[...] + gate_ref[0] * y


def _gla_core(x2d, bsz, seq, qkvo, la, w_out, mod, norm_g):
    t, d = x2d.shape
    tb = TB_GLA
    spb = seq // tb
    n_main, n_gate = qkvo.shape[1], la.shape[1]
    hv = GLA_HEADS * GLA_DV
    tok = lambda b, s: (b * spb + s, 0)
    return pl.pallas_call(
        _gla_core_kernel,
        grid=(bsz, spb),
        in_specs=[pl.BlockSpec((tb, n_main), tok),
                  pl.BlockSpec((tb, n_gate), tok),
                  pl.BlockSpec((tb, d), tok),
                  pl.BlockSpec((hv, d), lambda b, s: (0, 0)),
                  pl.BlockSpec((1, 1, d), lambda b, s: (b, 0, 2)),
                  pl.BlockSpec((1, GLA_DV), lambda b, s: (0, 0))],
        out_specs=pl.BlockSpec((tb, d), tok),
        out_shape=jax.ShapeDtypeStruct((t, d), F32),
        scratch_shapes=[pltpu.VMEM((GLA_HEADS, GLA_DV, GLA_DK), F32),
                        pltpu.VMEM((tb, hv), MXU_DTYPE)],
        compiler_params=_cparams("parallel", "arbitrary"),
        name="gla_core",
    )(qkvo, la, x2d, w_out.astype(MXU_DTYPE), mod, norm_g.reshape(1, GLA_DV))


def _sort16_pairs():
    def merge(lo, hi, r):
        step = r * 2
        if step < hi - lo:
            yield from merge(lo, hi, step)
            yield from merge(lo + r, hi, step)
            yield from [(i, i + r) for i in range(lo + r, hi - r, step)]
        else:
            yield (lo, lo + r)

    def sort(lo, hi):
        if hi - lo >= 1:
            mid = lo + (hi - lo) // 2
            yield from sort(lo, mid)
            yield from sort(mid + 1, hi)
            yield from merge(lo, hi, 1)

    return tuple(sort(0, PEER_TOPK - 1))


_SORT16 = _sort16_pairs()


def _sort16_desc(w):
    w = list(w)
    for i, j in _SORT16:
        w[i], w[j] = jnp.maximum(w[i], w[j]), jnp.minimum(w[i], w[j])
    return w


def _merge_top16(a, b):
    n = PEER_TOPK
    w = [jnp.maximum(a[i], b[n - 1 - i]) for i in range(n)]
    d = n // 2
    while d >= 1:
        for i in range(n):
            if i & d == 0:
                w[i], w[i + d] = jnp.maximum(w[i], w[i + d]), jnp.minimum(w[i], w[i + d])
        d //= 2
    return w


def _top16_desc(wires):
    n = PEER_TOPK
    runs = [_sort16_desc(wires[i:i + n]) for i in range(0, len(wires), n)]
    while len(runs) > 1:
        runs = [_merge_top16(runs[i], runs[i + 1]) for i in range(0, len(runs), 2)]
    return runs[0]


def _route_tile(s1, s2):
    n = PEER_TOPK
    v1, v2 = _top16_desc(s1), _top16_desc(s2)
    run = [v1[0] + v2[b] for b in range(n)]
    for a in range(1, n):
        run = _merge_top16(run, [v1[a] + v2[b] for b in range(n)])
    tau = run[n - 1]
    e1 = [jnp.exp(v - v1[0]) for v in v1]
    e2 = [jnp.exp(v - v2[0]) for v in v2]
    z = jnp.zeros_like(tau)
    thr_rank = []
    for a in range(n):
        za = jnp.zeros_like(tau)
        ta = jnp.full_like(tau, POS_INF)
        for b in range(n):
            sel = (v1[a] + v2[b]) >= tau
            za = za + jnp.where(sel, e2[b], 0.0)
            ta = jnp.where(sel, v2[b], ta)
        z = z + e1[a] * za
        thr_rank.append(ta)
    rz = 1.0 / z
    thr, coef = [], []
    for s in s1:
        t = jnp.full_like(tau, POS_INF)
        for a in range(n - 1, -1, -1):
            t = jnp.where(s == v1[a], thr_rank[a], t)
        thr.append(t)
        coef.append(jnp.exp(s - v1[0]) * rz * GELU_C)
    return thr, coef, v2[0]


def _peer_route_kernel(x_ref, g_ref, sh_ref, sc_ref, wqt_ref, k1i_ref, k2i_ref, k2n_ref,
                       h_ref, thr_ref, coef_ref, s2_ref, e2_ref):
    nk, nh = PEER_KEYS, PEER_HEADS
    h = _rms_modulate(x_ref[...], g_ref[...], sh_ref[0], sc_ref[0]).astype(MXU_DTYPE)
    h_ref[...] = h
    q_t = lax.dot_general(wqt_ref[...], h, _NT, preferred_element_type=F32).astype(MXU_DTYPE)
    half = q_t.shape[0] // 2
    s1i = jnp.dot(k1i_ref[...], q_t[:half], preferred_element_type=F32)
    s2i = jnp.dot(k2i_ref[...], q_t[half:], preferred_element_type=F32)
    s2n = jnp.dot(k2n_ref[...], q_t[half:], preferred_element_type=F32)
    for lt in range(thr_ref.shape[0]):
        ls = slice(lt * LANES, (lt + 1) * LANES)
        w1 = [s1i[k * nh:(k + 1) * nh, ls] for k in range(nk)]
        w2 = [s2i[k * nh:(k + 1) * nh, ls] for k in range(nk)]
        thr, coef, m2 = _route_tile(w1, w2)
        for k in range(nk):
            thr_ref[lt, k] = thr[k]
            coef_ref[lt, k] = coef[k]
        for hh in range(nh):
            s2h = s2n[hh * nk:(hh + 1) * nk, ls]
            s2_ref[lt, hh] = s2h
            e2_ref[lt, hh] = jnp.exp(s2h - m2[hh:hh + 1])


def _peer_route(x2d, seq, g, mod, w_q, subkeys):
    t, d = x2d.shape
    tb = TB_ROUTE
    spb = seq // tb
    nh, nk, half = PEER_HEADS, PEER_KEYS, PEER_QDIM // 2
    nq = nh * half
    wqt = w_q.reshape(d, nh, 2, half).transpose(2, 1, 3, 0).reshape(2 * nq, d).astype(MXU_DTYPE)
    eye = jnp.eye(nh, dtype=subkeys.dtype)
    interleaved = lambda k: (k[:, None, None, :] * eye[None, :, :, None]).reshape(nk * nh, nq).astype(MXU_DTYPE)
    natural = lambda k: (eye[:, None, :, None] * k[None, :, None, :]).reshape(nh * nk, nq).astype(MXU_DTYPE)
    const2 = lambda i: (0, 0)
    tiles = tb // LANES
    side_k = pl.BlockSpec((tiles, nk, nh, LANES), lambda i: (i, 0, 0, 0))
    side_h = pl.BlockSpec((tiles, nh, nk, LANES), lambda i: (i, 0, 0, 0))
    shape_k = jax.ShapeDtypeStruct((t // LANES, nk, nh, LANES), F32)
    shape_h = jax.ShapeDtypeStruct((t // LANES, nh, nk, LANES), F32)
    return pl.pallas_call(
        _peer_route_kernel,
        grid=(t // tb,),
        in_specs=[pl.BlockSpec((tb, d), lambda i: (i, 0)),
                  pl.BlockSpec((1, d), const2),
                  pl.BlockSpec((1, 1, d), lambda i: (i // spb, 0, 3)),
                  pl.BlockSpec((1, 1, d), lambda i: (i // spb, 0, 4)),
                  pl.BlockSpec((2 * nq, d), const2),
                  pl.BlockSpec((nk * nh, nq), const2),
                  pl.BlockSpec((nk * nh, nq), const2),
                  pl.BlockSpec((nh * nk, nq), const2)],
        out_specs=[pl.BlockSpec((tb, d), lambda i: (i, 0)), side_k, side_k, side_h, side_h],
        out_shape=[jax.ShapeDtypeStruct((t, d), MXU_DTYPE), shape_k, shape_k, shape_h, shape_h],
        compiler_params=_cparams("parallel"),
        name="peer_route",
    )(x2d, g.reshape(1, d), mod, mod, wqt, interleaved(subkeys[0]), interleaved(subkeys[1]),
      natural(subkeys[1]))


def _gelu_scaled(xs):
    return xs * (1.0 + lax.erf(xs))


def _peer_dense_kernel(h_ref, u_ref, vt_ref, thr_ref, coef_ref, s2_ref, e2_ref, x_ref, gate_ref, fg_ref,
                       o_ref, st_ref, w_ref, acc_ref, *, final_norm):
    c = pl.program_id(1)
    ec, tb = st_ref.shape
    groups = ec // PEER_KEYS

    @pl.when(c == 0)
    def _():
        acc_ref[...] = jnp.zeros_like(acc_ref)

    st_ref[...] = lax.dot_general(u_ref[...], h_ref[...], _NT, preferred_element_type=F32)

    def group_body(j, carry):
        i1 = c * groups + j
        r0 = pl.multiple_of(j * PEER_KEYS, PEER_KEYS)
        for lt in range(tb // LANES):
            ls = slice(lt * LANES, (lt + 1) * LANES)
            gw = jnp.zeros((PEER_KEYS, LANES), F32)
            for hh in range(PEER_HEADS):
                thr = thr_ref[lt, i1, pl.ds(hh, 8, stride=0), :]
                cf = coef_ref[lt, i1, pl.ds(hh, 8, stride=0), :]
                thr = jnp.concatenate([thr] * (PEER_KEYS // 8), axis=0)
                cf = jnp.concatenate([cf] * (PEER_KEYS // 8), axis=0)
                gw = gw + jnp.where(s2_ref[lt, hh] >= thr, e2_ref[lt, hh] * cf, 0.0)
            a = _gelu_scaled(st_ref[pl.ds(r0, PEER_KEYS), ls])
            w_ref[pl.ds(r0, PEER_KEYS), ls] = (a * gw).astype(w_ref.dtype)
        return carry

    lax.fori_loop(0, groups, group_body, 0)
    acc_ref[...] += jnp.dot(vt_ref[...], w_ref[...], preferred_element_type=F32)

    @pl.when(c == pl.num_programs(1) - 1)
    def _():
        y = x_ref[...] + gate_ref[0] * acc_ref[...].T
        if final_norm:
            y = y * lax.rsqrt(jnp.mean(y * y, axis=-1, keepdims=True) + EPS) * fg_ref[...]
        o_ref[...] = y


def _peer_dense(x2d, seq, h, side, mod, u_tab, v_tab, final_g, final_norm):
    t, d = x2d.shape
    tb, ec = TB_PEER, EC_PEER
    spb = seq // tb
    n_exp = u_tab.shape[0]
    u = (u_tab * GELU_C).astype(MXU_DTYPE)
    vt = v_tab.T.astype(MXU_DTYPE)
    thr, coef, s2, e2 = side
    tiles = tb // LANES
    side_k = pl.BlockSpec((tiles, PEER_KEYS, PEER_HEADS, LANES), lambda i, c: (i, 0, 0, 0))
    side_h = pl.BlockSpec((tiles, PEER_HEADS, PEER_KEYS, LANES), lambda i, c: (i, 0, 0, 0))
    return pl.pallas_call(
        functools.partial(_peer_dense_kernel, final_norm=final_norm),
        grid=(t // tb, n_exp // ec),
        in_specs=[pl.BlockSpec((tb, d), lambda i, c: (i, 0)),
                  pl.BlockSpec((ec, d), lambda i, c: (c, 0)),
                  pl.BlockSpec((d, ec), lambda i, c: (0, c)),
                  side_k, side_k, side_h, side_h,
                  pl.BlockSpec((tb, d), lambda i, c: (i, 0)),
                  pl.BlockSpec((1, 1, d), lambda i, c: (i // spb, 0, 5)),
                  pl.BlockSpec((1, d), lambda i, c: (0, 0))],
        out_specs=pl.BlockSpec((tb, d), lambda i, c: (i, 0)),
        out_shape=jax.ShapeDtypeStruct((t, d), F32),
        scratch_shapes=[pltpu.VMEM((ec, tb), F32),
                        pltpu.VMEM((ec, tb), MXU_DTYPE),
                        pltpu.VMEM((d, tb), F32)],
        compiler_params=_cparams("parallel", "arbitrary"),
        name="peer_dense",
    )(h, u, vt, thr, coef, s2, e2, x2d, mod, final_g.reshape(1, d))


def _peer(x2d, seq, g, mod, w_q, subkeys, u_tab, v_tab, final_g, final_norm):
    h, *side = _peer_route(x2d, seq, g, mod, w_q, subkeys)
    return _peer_dense(x2d, seq, h, side, mod, u_tab, v_tab, final_g, final_norm)


def _kvq_kernel(x_ref, gkv_ref, shkv_ref, sckv_ref, gq_ref, shq_ref, scq_ref, wkv_ref, wq_ref,
                cos_ref, sinm_ref, sinp_ref, k_ref, v_ref, q_ref):
    x = x_ref[...]
    cos, sinm, sinp = cos_ref[...], sinm_ref[...], sinp_ref[...]
    hk = _rms_modulate(x, gkv_ref[...], shkv_ref[0], sckv_ref[0])
    kv = _dot(hk, wkv_ref[...])
    nk = k_ref.shape[1]
    k_ref[...] = _apply_rope(kv[:, :nk], cos, sinm, sinp).astype(k_ref.dtype)
    v_ref[...] = kv[:, nk:].astype(v_ref.dtype)
    hq = _rms_modulate(x, gq_ref[...], shq_ref[0], scq_ref[0])
    q_ref[...] = (_apply_rope(_dot(hq, wq_ref[...]), cos, sinm, sinp) * (SWA_HD ** -0.5)).astype(q_ref.dtype)


def _kvq(x2d, seq, kv_norm_g, kv_mod, kv_w, q_norm_g, mod, w_q, rope):
    t, d = x2d.shape
    tb = TB_PROJ
    spb = seq // tb
    nk = SWA_KV_HEADS * SWA_HD
    nq = SWA_HEADS * SWA_HD
    const = lambda i: (0, 0)
    tok = lambda w: pl.BlockSpec((tb, w), lambda i: (i, 0))
    modspec = lambda k: pl.BlockSpec((1, 1, d), lambda i: (i // spb, 0, k))
    return pl.pallas_call(
        _kvq_kernel,
        grid=(t // tb,),
        in_specs=[tok(d),
                  pl.BlockSpec((1, d), const), modspec(0), modspec(1),
                  pl.BlockSpec((1, d), const), modspec(0), modspec(1),
                  pl.BlockSpec((d, 2 * nk), const), pl.BlockSpec((d, nq), const),
                  tok(LANES), tok(LANES), tok(LANES)],
        out_specs=[tok(nk), tok(nk), tok(nq)],
        out_shape=[jax.ShapeDtypeStruct((t, nk), MXU_DTYPE), jax.ShapeDtypeStruct((t, nk), MXU_DTYPE),
                   jax.ShapeDtypeStruct((t, nq), MXU_DTYPE)],
        compiler_params=_cparams("parallel"),
        name="kv_q_proj",
    )(x2d, kv_norm_g.reshape(1, d), kv_mod, kv_mod, q_norm_g.reshape(1, d), mod, mod,
      kv_w.astype(MXU_DTYPE), w_q.astype(MXU_DTYPE), *rope)


def _swa_kernel(sink_ref, q_ref, kc_ref, kp_ref, vc_ref, vp_ref, x_ref, wout_ref, gate_ref,
                o_ref, att_ref):
    p = SWA_WINDOW
    group = SWA_HEADS // SWA_KV_HEADS
    first = pl.program_id(1) == 0
    lane = lax.broadcasted_iota(jnp.int32, (p, LANES), 1)
    rows = 2 * p
    qi = lax.broadcasted_iota(jnp.int32, (rows, 2 * p), 0) % p
    mi = lax.broadcasted_iota(jnp.int32, (rows, 2 * p), 1)
    band = (mi > qi) & (mi <= qi + p)
    top_half = lax.broadcasted_iota(jnp.int32, (rows, 1), 0) < p
    zero = jnp.zeros((), MXU_DTYPE)
    for jb in range(q_ref.shape[0] // p):
        rs = slice(jb * p, (jb + 1) * p)
        if jb == 0:
            k_prev, v_prev = kp_ref[...], vp_ref[...]
            mask = band & ((mi >= p) | jnp.logical_not(first))
        else:
            ps = slice((jb - 1) * p, jb * p)
            k_prev, v_prev = kc_ref[ps, :], vc_ref[ps, :]
            mask = band
        k_band = jnp.concatenate([k_prev, kc_ref[rs, :]], axis=0)
        v_band = jnp.concatenate([v_prev, vc_ref[rs, :]], axis=0)
        for kh in range(SWA_KV_HEADS):
            kv_tile = k_band[:, (kh // 2) * LANES:(kh // 2 + 1) * LANES]
            vv_tile = v_band[:, (kh // 2) * LANES:(kh // 2 + 1) * LANES]
            mine = (lane[0:1] // SWA_HD) == (kh % 2)
            k_own = jnp.where(mine, kv_tile, zero)
            v_own = jnp.where(mine, vv_tile, zero)
            k_swp = pltpu.roll(k_own.astype(F32), SWA_HD, 1).astype(MXU_DTYPE)
            v_swp = pltpu.roll(v_own.astype(F32), SWA_HD, 1).astype(MXU_DTYPE)
            k_lo, k_hi = (k_own, k_swp) if kh % 2 == 0 else (k_swp, k_own)
            v_lo, v_hi = (v_own, v_swp) if kh % 2 == 0 else (v_swp, v_own)
            t0 = kh * group // 2
            q2 = jnp.concatenate([q_ref[rs, t0 * LANES:(t0 + 1) * LANES],
                                  q_ref[rs, (t0 + 1) * LANES:(t0 + 2) * LANES]], axis=0)
            out = jnp.zeros((rows, LANES), F32)
            for pos, (k_x, v_x) in enumerate(((k_lo, v_lo), (k_hi, v_hi))):
                s = jnp.where(mask, _dot_nt(q2, k_x), NEG_INF)
                sink = jnp.where(top_half, sink_ref[kh * group + pos], sink_ref[kh * group + 2 + pos])
                m = jnp.maximum(jnp.max(s, axis=-1, keepdims=True), sink)
                e = jnp.exp(s - m)
                denom = jnp.sum(e, axis=-1, keepdims=True) + jnp.exp(sink - m)
                out = out + _dot(e, v_x) * (1.0 / denom)
            att_ref[rs, t0 * LANES:(t0 + 1) * LANES] = out[:p].astype(att_ref.dtype)
            att_ref[rs, (t0 + 1) * LANES:(t0 + 2) * LANES] = out[p:].astype(att_ref.dtype)
    y = jnp.dot(att_ref[...], wout_ref[...], preferred_element_type=F32)
    o_ref[...] = x_ref[...] + gate_ref[0] * y


def _swa(x2d, bsz, seq, q, k, v, sinks, w_out, mod):
    t, d = x2d.shape
    tb, p = TB_SWA, SWA_WINDOW
    spb = seq // tb
    nk, nq = k.shape[1], q.shape[1]
    tok = lambda b, s: (b * spb + s, 0)
    prev = lambda b, s: (jnp.maximum((b * spb + s) * (tb // p) - 1, 0), 0)
    return pl.pallas_call(
        _swa_kernel,
        grid=(bsz, spb),
        in_specs=[pl.BlockSpec(memory_space=pltpu.SMEM),
                  pl.BlockSpec((tb, nq), tok),
                  pl.BlockSpec((tb, nk), tok), pl.BlockSpec((p, nk), prev),
                  pl.BlockSpec((tb, nk), tok), pl.BlockSpec((p, nk), prev),
                  pl.BlockSpec((tb, d), tok),
                  pl.BlockSpec((nq, d), lambda b, s: (0, 0)),
                  pl.BlockSpec((1, 1, d), lambda b, s: (b, 0, 2))],
        out_specs=pl.BlockSpec((tb, d), tok),
        out_shape=jax.ShapeDtypeStruct((t, d), F32),
        scratch_shapes=[pltpu.VMEM((tb, nq), MXU_DTYPE)],
        compiler_params=_cparams("parallel", "parallel"),
        name="swa_attention",
    )(sinks, q, k, k, v, v, x2d, w_out.astype(MXU_DTYPE), mod)


def kernel(x, c, positions, mod_w, mod_b, norm_g, gla_w_in, gla_w_g2, gla_b_g2, gla_norm_g, gla_w_out,
           kv_mod_w, kv_mod_b, kv_norm_g, kv_w, swa_w_q, swa_sinks, swa_w_out, peer_w_q, peer_subkeys,
           peer_u, peer_v, final_norm_g):
    bsz, seq, d = x.shape
    x2d = x.reshape(bsz * seq, d)
    mod0 = _mod_vectors(c, mod_w[0], mod_b[0]).reshape(bsz, 1, 6 * d)
    mod1 = _mod_vectors(c, mod_w[1], mod_b[1]).reshape(bsz, 1, 6 * d)
    kv_mod = _mod_vectors(c, kv_mod_w, kv_mod_b).reshape(bsz, 1, 2 * d)
    rope = _rope_tables(positions)

    qkvo, la = _gla_in(x2d, seq, norm_g[0, 0], mod0, gla_w_in[0], gla_w_g2[0], gla_b_g2[0])
    x2d = _gla_core(x2d, bsz, seq, qkvo, la, gla_w_out[0], mod0, gla_norm_g[0])
    x2d = _peer(x2d, seq, norm_g[0, 1], mod0, peer_w_q[0], peer_subkeys[0], peer_u[0], peer_v[0],
                final_norm_g, False)

    k, v, q = _kvq(x2d, seq, kv_norm_g, kv_mod, kv_w, norm_g[1, 0], mod1, swa_w_q[0], rope)
    x2d = _swa(x2d, bsz, seq, q, k, v, swa_sinks[0], swa_w_out[0], mod1)
    x2d = _peer(x2d, seq, norm_g[1, 1], mod1, peer_w_q[1], peer_subkeys[1], peer_u[1], peer_v[1],
                final_norm_g, True)
    return x2d.reshape(bsz, seq, d)
```
